```python
import math
import jax, jax.numpy as jnp
from jax import lax
import numpy as np

D_MODEL = 2048
BATCH = 4
SEQ = 2048
DEPTH = 1
DEC_BATCH = 128
DEC_SEQ = 8
PAST_LEN = 16384
PAGE_SIZE = 128

LRU_WIDTH = D_MODEL // 2
LRU_BLOCKS = 8
LRU_BLOCK_W = LRU_WIDTH // LRU_BLOCKS
CONV_W = 4
LRU_C = 8.0
N_HEADS = 8
QK_NOPE = 128
QK_ROPE = 64
QK_HEAD = QK_NOPE + QK_ROPE
V_HEAD = 128
Q_LORA = 768
KV_LORA = 512
MLA_WIDTH = N_HEADS * V_HEAD
MIX_WIDTH = LRU_WIDTH + MLA_WIDTH
IN_WIDTH = 2 * LRU_WIDTH + Q_LORA + KV_LORA + QK_ROPE
ROPE_THETA = 10000.0
Q_BLOCK = 128
N_EXPERTS = 32
TOP_K = 4
D_EXPERT = D_MODEL
SWIGLU_LIMIT = 7.0
SWIGLU_ALPHA = 1.702
EPS = 1e-6

kernel_name = "hymba_rglru_mla_moe_step"


def _rmsnorm(x, gain):
    xf = x.astype(jnp.float32)
    inv = lax.rsqrt(jnp.mean(xf * xf, axis=-1, keepdims=True) + EPS)
    return (xf * inv).astype(x.dtype) * gain


def _rope(x, pos):
    half = QK_ROPE // 2
    inv_freq = jnp.power(ROPE_THETA, -jnp.arange(half, dtype=jnp.float32) / half)
    ang = pos.astype(jnp.float32)[:, None] * inv_freq[None, :]
    shape = (ang.shape[0],) + (1,) * (x.ndim - 3) + (half,)
    cos = jnp.cos(ang).reshape(shape).astype(x.dtype)
    sin = jnp.sin(ang).reshape(shape).astype(x.dtype)
    x1, x2 = x[..., :half], x[..., half:]
    return jnp.concatenate([x1 * cos - x2 * sin, x2 * cos + x1 * sin], axis=-1)


def _split_in(x, norm_gain, w_in):
    z = _rmsnorm(x, norm_gain) @ w_in
    offs = [LRU_WIDTH, 2 * LRU_WIDTH, 2 * LRU_WIDTH + Q_LORA, 2 * LRU_WIDTH + Q_LORA + KV_LORA]
    return jnp.split(z, offs, axis=-1)


def _causal_conv(x, buf, w, b):
    T = x.shape[1]
    xp = jnp.concatenate([buf, x], axis=1)
    y = b + sum(xp[:, k:k + T] * w[k] for k in range(CONV_W))
    return y, xp[:, T:]


def _rglru(x, h0, w_a, b_a, w_i, b_i, lam):
    B, T, W = x.shape
    xb = x.reshape(B, T, LRU_BLOCKS, LRU_BLOCK_W)
    r = jax.nn.sigmoid((jnp.einsum('btnd,nde->btne', xb, w_a).reshape(B, T, W) + b_a).astype(jnp.float32))
    i = jax.nn.sigmoid((jnp.einsum('btnd,nde->btne', xb, w_i).reshape(B, T, W) + b_i).astype(jnp.float32))
    log_a = -LRU_C * r * jax.nn.softplus(-lam.astype(jnp.float32))
    a = jnp.exp(log_a)
    mult = jnp.sqrt(jnp.maximum(-jnp.expm1(2.0 * log_a), 0.0))
    u = mult * i * x.astype(jnp.float32)

    def step(h, au):
        a_t, u_t = au
        h = a_t * h + u_t
        return h, h

    hT, hs = lax.scan(step, h0.astype(jnp.float32), (jnp.swapaxes(a, 0, 1), jnp.swapaxes(u, 0, 1)))
    return jnp.swapaxes(hs, 0, 1).astype(x.dtype), hT.astype(x.dtype)


def _recurrent(x_rg, gate_rg, conv_buf, h0, conv_w, conv_b, w_a, b_a, w_i, b_i, lam):
    xc, new_buf = _causal_conv(x_rg, conv_buf, conv_w, conv_b)
    hs, hT = _rglru(xc, h0, w_a, b_a, w_i, b_i, lam)
    return jax.nn.gelu(gate_rg) * hs, new_buf, hT


def _mla_project(q_lat, kv_lat, k_r, pos, q_norm_gain, w_uq, kv_norm_gain, w_uk, qk_gain_q, qk_gain_k):
    B, T, _ = q_lat.shape
    q = (_rmsnorm(q_lat, q_norm_gain) @ w_uq).reshape(B, T, N_HEADS, QK_HEAD)
    q = _rmsnorm(q, qk_gain_q)
    q_nope, q_pe = q[..., :QK_NOPE], _rope(q[..., QK_NOPE:], pos)
    q_abs = jnp.einsum('bthd,chd->bthc', q_nope * qk_gain_k[:QK_NOPE], w_uk)
    c = _rmsnorm(kv_lat, kv_norm_gain)
    k_nope = jnp.einsum('btc,chd->bthd', c, w_uk)
    ms = (jnp.sum(jnp.square(k_nope.astype(jnp.float32)), axis=-1)
          + jnp.sum(jnp.square(k_r.astype(jnp.float32)), axis=-1)[..., None]) / QK_HEAD
    k_scale = lax.rsqrt(ms + EPS).astype(c.dtype)
    k_pe = _rope(k_r * qk_gain_k[QK_NOPE:], pos)
    return q_abs, q_pe, c, k_pe, k_scale


def _scores(q_abs, q_pe, c, k_pe, k_scale):
    s = jnp.einsum('bthc,bsc->bhts', q_abs, c) + jnp.einsum('bthr,bsr->bhts', q_pe, k_pe)
    ks = jnp.swapaxes(k_scale, 1, 2).astype(jnp.float32)[:, :, None, :]
    return s.astype(jnp.float32) * ks / math.sqrt(QK_HEAD)


def _mla_prompt(q_abs, q_pe, c, k_pe, k_scale):
    B, S, H, C = q_abs.shape
    nb = S // Q_BLOCK
    qa = jnp.moveaxis(q_abs.reshape(B, nb, Q_BLOCK, H, C), 1, 0)
    qp = jnp.moveaxis(q_pe.reshape(B, nb, Q_BLOCK, H, QK_ROPE), 1, 0)
    kpos = jnp.arange(S)
    cf = c.astype(jnp.float32)

    def block(args):
        qa_b, qp_b, j = args
        s = _scores(qa_b, qp_b, c, k_pe, k_scale)
        qpos = j * Q_BLOCK + jnp.arange(Q_BLOCK)
        s = jnp.where(kpos[None, :] <= qpos[:, None], s, -jnp.inf)
        p = jax.nn.softmax(s, axis=-1)
        return jnp.einsum('bhts,bsc->bthc', p, cf)

    o = lax.map(block, (qa, qp, jnp.arange(nb)))
    return jnp.moveaxis(o, 0, 1).reshape(B, S, H, C)


def _mla_sample(q_abs, q_pe, c_new, kpe_new, ks_new, cache_ckv, cache_kpe, cache_kscale, page_table):
    B, T, H, C = q_abs.shape

    def page_step(carry, phys):
        m, l, acc = carry
        c = cache_ckv[phys]
        s = _scores(q_abs, q_pe, c, cache_kpe[phys], cache_kscale[phys])
        m_new = jnp.maximum(m, jnp.max(s, axis=-1))
        corr = jnp.exp(m - m_new)
        p = jnp.exp(s - m_new[..., None])
        l = l * corr + jnp.sum(p, axis=-1)
        acc = acc * corr[..., None] + jnp.einsum('bhts,bsc->bhtc', p, c.astype(jnp.float32))
        return (m_new, l, acc), None

    init = (jnp.full((B, H, T), -jnp.inf, jnp.float32), jnp.zeros((B, H, T), jnp.float32),
            jnp.zeros((B, H, T, C), jnp.float32))
    (m, l, acc), _ = lax.scan(page_step, init, page_table.T)
    s = _scores(q_abs, q_pe, c_new, kpe_new, ks_new)
    s = jnp.where(jnp.tril(jnp.ones((T, T), dtype=bool)), s, -jnp.inf)
    m_new = jnp.maximum(m, jnp.max(s, axis=-1))
    corr = jnp.exp(m - m_new)
    p = jnp.exp(s - m_new[..., None])
    l = l * corr + jnp.sum(p, axis=-1)
    acc = acc * corr[..., None] + jnp.einsum('bhts,bsc->bhtc', p, c_new.astype(jnp.float32))
    return jnp.swapaxes(acc / l[..., None], 1, 2)


def _merge(x, y_rg, o_lat, w_uv, out_norm_rg, out_norm_mla, w_out):
    B, T, _ = x.shape
    o = jnp.einsum('bthc,chd->bthd', o_lat.astype(x.dtype), w_uv).reshape(B, T, MLA_WIDTH)
    y = jnp.concatenate([_rmsnorm(y_rg, out_norm_rg), _rmsnorm(o, out_norm_mla)], axis=-1)
    return x + y @ w_out


def _moe(h, norm_gain, w_router, b_router, w_gate, b_gate, w_up, b_up, w_down, b_down):
    xn = _rmsnorm(h, norm_gain)
    logits = (xn @ w_router + b_router).astype(jnp.float32)
    top_v, top_i = lax.top_k(logits, TOP_K)
    gates = jax.nn.softmax(top_v, axis=-1)
    dense_gate = jnp.sum(jax.nn.one_hot(top_i, N_EXPERTS, dtype=jnp.float32) * gates[..., None], axis=-2)
    out = jnp.zeros(h.shape, jnp.float32)
    for e in range(N_EXPERTS):
        g = jnp.minimum(xn @ w_gate[e] + b_gate[e], SWIGLU_LIMIT)
        u = jnp.clip(xn @ w_up[e] + b_up[e], -SWIGLU_LIMIT, SWIGLU_LIMIT)
        act = (u + 1.0) * g * jax.nn.sigmoid(SWIGLU_ALPHA * g)
        out = out + dense_gate[:, e:e + 1] * (act @ w_down[e] + b_down[e]).astype(jnp.float32)
    return h + out.astype(h.dtype)


def setup_inputs(seed: int = 0) -> dict:
    key = jax.random.key(seed)
    ks = jax.random.split(key, 40)
    n_pages = PAST_LEN // PAGE_SIZE
    n_phys = (DEC_BATCH * n_pages * 5) // 4

    def nrm(k, shape, scale):
        return jax.random.normal(k, shape, jnp.float32) * scale

    def gain(k, n):
        return 1.0 + 0.05 * jax.random.normal(k, (n,), jnp.float32)

    perm = jax.random.permutation(ks[0], n_phys)
    page_table = perm[: DEC_BATCH * n_pages].reshape(DEC_BATCH, n_pages).astype(jnp.int32)
    a0 = jax.random.uniform(ks[1], (LRU_WIDTH,), jnp.float32, 0.9, 0.999)
    return {
        "x_prompt": nrm(ks[2], (BATCH, SEQ, D_MODEL), 1.0),
        "x_sample": nrm(ks[3], (DEC_BATCH, DEC_SEQ, D_MODEL), 1.0),
        "cache_ckv": nrm(ks[4], (n_phys, PAGE_SIZE, KV_LORA), 1.0),
        "cache_kpe": nrm(ks[5], (n_phys, PAGE_SIZE, QK_ROPE), 1.0),
        "cache_kscale": jax.random.uniform(ks[6], (n_phys, PAGE_SIZE, N_HEADS), jnp.float32, 0.8, 1.2),
        "state_conv": nrm(ks[7], (DEC_BATCH, CONV_W - 1, LRU_WIDTH), 1.0),
        "state_h": nrm(ks[8], (DEC_BATCH, LRU_WIDTH), 0.5),
        "page_table": page_table,
        "norm_attn_gain": gain(ks[9], D_MODEL),
        "w_in": nrm(ks[10], (D_MODEL, IN_WIDTH), D_MODEL ** -0.5),
        "conv_w": nrm(ks[11], (CONV_W, LRU_WIDTH), CONV_W ** -0.5),
        "conv_b": nrm(ks[12], (LRU_WIDTH,), 0.01),
        "lru_w_a": nrm(ks[13], (LRU_BLOCKS, LRU_BLOCK_W, LRU_BLOCK_W), LRU_BLOCK_W ** -0.5),
        "lru_b_a": nrm(ks[14], (LRU_WIDTH,), 0.01),
        "lru_w_i": nrm(ks[15], (LRU_BLOCKS, LRU_BLOCK_W, LRU_BLOCK_W), LRU_BLOCK_W ** -0.5),
        "lru_b_i": nrm(ks[16], (LRU_WIDTH,), 0.01),
        "lru_lambda": jnp.log(a0) - jnp.log1p(-a0),
        "q_norm_gain": gain(ks[17], Q_LORA),
        "w_uq": nrm(ks[18], (Q_LORA, N_HEADS * QK_HEAD), Q_LORA ** -0.5),
        "kv_norm_gain": gain(ks[19], KV_LORA),
        "w_uk": nrm(ks[20], (KV_LORA, N_HEADS, QK_NOPE), KV_LORA ** -0.5),
        "w_uv": nrm(ks[21], (KV_LORA, N_HEADS, V_HEAD), KV_LORA ** -0.5),
        "qk_gain_q": gain(ks[22], QK_HEAD),
        "qk_gain_k": gain(ks[23], QK_HEAD),
        "out_norm_rg": gain(ks[24], LRU_WIDTH),
        "out_norm_mla": gain(ks[25], MLA_WIDTH),
        "w_out": nrm(ks[26], (MIX_WIDTH, D_MODEL), MIX_WIDTH ** -0.5),
        "norm_ffn_gain": gain(ks[27], D_MODEL),
        "w_router": nrm(ks[28], (D_MODEL, N_EXPERTS), D_MODEL ** -0.5),
        "b_router": nrm(ks[29], (N_EXPERTS,), 0.01),
        "w_gate": nrm(ks[30], (N_EXPERTS, D_MODEL, D_EXPERT), D_MODEL ** -0.5),
        "b_gate": nrm(ks[31], (N_EXPERTS, D_EXPERT), 0.01),
        "w_up": nrm(ks[32], (N_EXPERTS, D_MODEL, D_EXPERT), D_MODEL ** -0.5),
        "b_up": nrm(ks[33], (N_EXPERTS, D_EXPERT), 0.01),
        "w_down": nrm(ks[34], (N_EXPERTS, D_EXPERT, D_MODEL), D_EXPERT ** -0.5),
        "b_down": nrm(ks[35], (N_EXPERTS, D_MODEL), 0.01),
    }


def reference(x_prompt, x_sample, cache_ckv, cache_kpe, cache_kscale, state_conv, state_h, page_table,
              norm_attn_gain, w_in, conv_w, conv_b, lru_w_a, lru_b_a, lru_w_i, lru_b_i, lru_lambda,
              q_norm_gain, w_uq, kv_norm_gain, w_uk, w_uv, qk_gain_q, qk_gain_k,
              out_norm_rg, out_norm_mla, w_out, norm_ffn_gain, w_router, b_router,
              w_gate, b_gate, w_up, b_up, w_down, b_down):
    B, S, D = x_prompt.shape
    DB, T, _ = x_sample.shape
    past_len = page_table.shape[1] * PAGE_SIZE
    pos_p = jnp.arange(S)
    pos_s = past_len + jnp.arange(T)
    xp, xs = x_prompt, x_sample
    for _ in range(DEPTH):
        xr_p, gt_p, ql_p, kvl_p, kr_p = _split_in(xp, norm_attn_gain, w_in)
        y_rg_p, p_conv, p_h = _recurrent(xr_p, gt_p, jnp.zeros((B, CONV_W - 1, LRU_WIDTH), xp.dtype),
                                         jnp.zeros((B, LRU_WIDTH), xp.dtype), conv_w, conv_b,
                                         lru_w_a, lru_b_a, lru_w_i, lru_b_i, lru_lambda)
        qa_p, qpe_p, p_ckv, p_kpe, p_kscale = _mla_project(ql_p, kvl_p, kr_p, pos_p, q_norm_gain, w_uq,
                                                           kv_norm_gain, w_uk, qk_gain_q, qk_gain_k)
        o_p = _mla_prompt(qa_p, qpe_p, p_ckv, p_kpe, p_kscale)
        h_p = _merge(xp, y_rg_p, o_p, w_uv, out_norm_rg, out_norm_mla, w_out)
        xr_s, gt_s, ql_s, kvl_s, kr_s = _split_in(xs, norm_attn_gain, w_in)
        y_rg_s, s_conv, s_h = _recurrent(xr_s, gt_s, state_conv, state_h, conv_w, conv_b,
                                         lru_w_a, lru_b_a, lru_w_i, lru_b_i, lru_lambda)
        qa_s, qpe_s, s_ckv, s_kpe, s_kscale = _mla_project(ql_s, kvl_s, kr_s, pos_s, q_norm_gain, w_uq,
                                                           kv_norm_gain, w_uk, qk_gain_q, qk_gain_k)
        o_s = _mla_sample(qa_s, qpe_s, s_ckv, s_kpe, s_kscale, cache_ckv, cache_kpe, cache_kscale, page_table)
        h_s = _merge(xs, y_rg_s, o_s, w_uv, out_norm_rg, out_norm_mla, w_out)
        h_all = jnp.concatenate([h_p.reshape(B * S, D), h_s.reshape(DB * T, D)], axis=0)
        y_all = _moe(h_all, norm_ffn_gain, w_router, b_router, w_gate, b_gate, w_up, b_up, w_down, b_down)
        xp = y_all[: B * S].reshape(B, S, D)
        xs = y_all[B * S:].reshape(DB, T, D)
    return (xp, xs, p_ckv, p_kpe, p_kscale, p_conv, p_h, s_ckv, s_kpe, s_kscale, s_conv, s_h)
```

```python
import functools
import math

import jax
import jax.numpy as jnp
from jax import lax
from jax.experimental import pallas as pl
from jax.experimental.pallas import tpu as pltpu

F32 = jnp.float32
BF16 = jnp.bfloat16
I32 = jnp.int32

EPS = 1e-6
LRU_C = 8.0
LRU_BLOCK_W = 128
CONV_W = 4
ROPE_THETA = 10000.0
QK_NOPE = 128
TOP_K = 4
SWIGLU_LIMIT = 7.0
SWIGLU_ALPHA = 1.702

SUBLANES = 8
LANES = 128
VMEM_LIMIT_BYTES = 56 * 1024 * 1024

TOKEN_TILE = 256
SCAN_TILE = 256
ATTN_TQ = 256
ATTN_TK = 512
PAGES_PER_GROUP = 8
MOE_SUB = 256
MOE_CAP_SUB = 8
MOE_F = 256
COMBINE_TILE = 128


def _params(sem, vmem=VMEM_LIMIT_BYTES):
    return pltpu.CompilerParams(dimension_semantics=sem, vmem_limit_bytes=vmem)


def _dot(a, b):
    return jnp.dot(a, b, preferred_element_type=F32)


def _dot_nt(a, b):
    return lax.dot_general(a, b, (((1,), (1,)), ((), ())), preferred_element_type=F32)


def _rms(x):
    return x * lax.rsqrt(jnp.mean(x * x, axis=-1, keepdims=True) + EPS)


def _const_spec(shape):
    nd = len(shape)
    return pl.BlockSpec(shape, lambda *_: (0,) * nd)


def _inproj_body(x_ref, g_ref, w_ref, *out_refs, offs):
    xn = (_rms(x_ref[...]) * g_ref[...]).astype(BF16)
    for o_ref, (a, b) in zip(out_refs, offs):
        o_ref[...] = _dot(xn, w_ref[:, a:b])


def _in_proj(x, gain, w_bf, widths):
    n, d = x.shape
    tm = min(TOKEN_TILE, n)
    offs, a = [], 0
    for w in widths:
        offs.append((a, a + w))
        a += w
    return pl.pallas_call(
        functools.partial(_inproj_body, offs=tuple(offs)),
        grid=(n // tm,),
        in_specs=[pl.BlockSpec((tm, d), lambda i: (i, 0)), _const_spec((1, d)), _const_spec(w_bf.shape)],
        out_specs=[pl.BlockSpec((tm, w), lambda i: (i, 0)) for w in widths],
        out_shape=[jax.ShapeDtypeStruct((n, w), F32) for w in widths],
        compiler_params=_params(("parallel",)),
        name="in_proj",
    )(x, gain.reshape(1, d), w_bf)


def _mla_proj_body(ql_ref, kvl_ref, kr_ref, cos_ref, sin_ref, qng_ref, wqn_ref, wqr_ref, gqn_ref, gqr_ref,
                   gkn_ref, gkr_ref, kvg_ref, wukt_ref, wuk_ref,
                   qabs_ref, qpe_ref, c_ref, kpe_ref, ks_ref, *, nh):
    tm = ql_ref.shape[0]
    rope = kr_ref.shape[1]
    half = rope // 2
    qk_head = QK_NOPE + rope
    g8 = tm // SUBLANES

    qn = (_rms(ql_ref[...]) * qng_ref[...]).astype(BF16)
    qnope = _dot(qn, wqn_ref[...])
    qrope = _dot(qn, wqr_ref[...])
    r2 = qrope * qrope
    lane = lax.broadcasted_iota(I32, (tm, LANES), 1)
    heads_per_vreg = LANES // rope
    inv_heads = []
    for h in range(nh):
        blk = qnope[:, h * QK_NOPE:(h + 1) * QK_NOPE]
        ssn = jnp.sum(blk * blk, axis=-1, keepdims=True)
        v, sub = divmod(h, heads_per_vreg)
        rblk = r2[:, v * LANES:(v + 1) * LANES]
        in_head = (lane >= sub * rope) & (lane < (sub + 1) * rope)
        ssr = jnp.sum(jnp.where(in_head, rblk, 0.0), axis=-1, keepdims=True)
        inv_h = lax.rsqrt((ssn + ssr) / qk_head + EPS)
        inv_heads.append(inv_h)
        qa_in = ((blk * inv_h) * gqn_ref[...]) * gkn_ref[...]
        qabs = _dot(qa_in.astype(BF16), wukt_ref[h])
        qabs_ref[:, h] = qabs.reshape(g8, SUBLANES, qabs.shape[1])

    cols = []
    for v in range(nh // heads_per_vreg):
        col = inv_heads[v * heads_per_vreg]
        for sub in range(1, heads_per_vreg):
            col = jnp.where(lane >= sub * rope, inv_heads[v * heads_per_vreg + sub], col)
        cols.append(col)
    inv_all = jnp.concatenate(cols, axis=1)
    qr = (qrope * inv_all) * gqr_ref[...]
    wide = lax.broadcasted_iota(I32, qr.shape, 1)
    first_half = (wide % rope) < half
    width = qr.shape[1]
    rot = jnp.where(first_half, pltpu.roll(qr, width - half, 1), pltpu.roll(qr, half, 1))
    qpe = qr * cos_ref[...] + rot * sin_ref[...]
    for h in range(nh):
        qpe_ref[:, h] = qpe[:, h * rope:(h + 1) * rope].reshape(g8, SUBLANES, rope)

    c = _rms(kvl_ref[...]) * kvg_ref[...]
    c_ref[...] = c
    kn = _dot(c.astype(BF16), wuk_ref[...])
    kr = kr_ref[...]
    sskr = jnp.sum(kr * kr, axis=-1, keepdims=True)
    lane_h = lax.broadcasted_iota(I32, (tm, nh), 1)
    ks = jnp.zeros((tm, nh), F32)
    for h in range(nh):
        blk = kn[:, h * QK_NOPE:(h + 1) * QK_NOPE]
        ssk = jnp.sum(blk * blk, axis=-1, keepdims=True)
        ks = jnp.where(lane_h == h, lax.rsqrt((ssk + sskr) / qk_head + EPS), ks)
    ks_ref[...] = ks
    krg = kr * gkr_ref[...]
    swapped = jnp.concatenate([krg[:, half:], krg[:, :half]], axis=1)
    kpe_ref[...] = krg * cos_ref[:, :rope] + swapped * sin_ref[:, :rope]


def _mla_proj(ql, kvl, kr, cos_w, sin_w, q_norm_gain, wqn_bf, wqr_bf, gqn, gqr_w, gkn, gkr, kv_norm_gain,
              wukt_bf, wuk_bf, nh):
    n, q_lora = ql.shape
    kv_lora = kvl.shape[1]
    rope = kr.shape[1]
    tm = min(TOKEN_TILE, n)
    g8 = tm // SUBLANES
    row = lambda i: (i, 0)
    row4 = lambda i: (i, 0, 0, 0)
    return pl.pallas_call(
        functools.partial(_mla_proj_body, nh=nh),
        grid=(n // tm,),
        in_specs=[pl.BlockSpec((tm, q_lora), row), pl.BlockSpec((tm, kv_lora), row), pl.BlockSpec((tm, rope), row),
                  pl.BlockSpec((tm, nh * rope), row), pl.BlockSpec((tm, nh * rope), row),
                  _const_spec((1, q_lora)), _const_spec(wqn_bf.shape), _const_spec(wqr_bf.shape),
                  _const_spec((1, QK_NOPE)), _const_spec((1, nh * rope)), _const_spec((1, QK_NOPE)),
                  _const_spec((1, rope)), _const_spec((1, kv_lora)), _const_spec(wukt_bf.shape),
                  _const_spec(wuk_bf.shape)],
        out_specs=[pl.BlockSpec((g8, nh, SUBLANES, kv_lora), row4), pl.BlockSpec((g8, nh, SUBLANES, rope), row4),
                   pl.BlockSpec((tm, kv_lora), row), pl.BlockSpec((tm, rope), row), pl.BlockSpec((tm, nh), row)],
        out_shape=[jax.ShapeDtypeStruct((n // SUBLANES, nh, SUBLANES, kv_lora), F32),
                   jax.ShapeDtypeStruct((n // SUBLANES, nh, SUBLANES, rope), F32),
                   jax.ShapeDtypeStruct((n, kv_lora), F32), jax.ShapeDtypeStruct((n, rope), F32),
                   jax.ShapeDtypeStruct((n, nh), F32)],
        compiler_params=_params(("parallel",)),
        name="mla_proj",
    )(ql, kvl, kr, cos_w, sin_w, q_norm_gain.reshape(1, -1), wqn_bf, wqr_bf, gqn.reshape(1, -1),
      gqr_w.reshape(1, -1), gkn.reshape(1, -1), gkr.reshape(1, -1), kv_norm_gain.reshape(1, -1), wukt_bf, wuk_bf)


def _expm1(x):
    u = jnp.exp(x)
    um1 = u - 1.0
    return jnp.where(u == 1.0, x, jnp.where(um1 == -1.0, -1.0, um1 * x / jnp.log(u)))


def _lru_gates(xc, wai_ref, ba_ref, bi_ref, lam_ref):
    nblk = wai_ref.shape[0]
    xcb = xc.astype(BF16)
    ra, ri = [], []
    for n in range(nblk):
        g = _dot(xcb[:, n * LRU_BLOCK_W:(n + 1) * LRU_BLOCK_W], wai_ref[n])
        ra.append(g[:, :LRU_BLOCK_W])
        ri.append(g[:, LRU_BLOCK_W:])
    r = jax.nn.sigmoid(jnp.concatenate(ra, axis=1) + ba_ref[...])
    i = jax.nn.sigmoid(jnp.concatenate(ri, axis=1) + bi_ref[...])
    log_a = (-LRU_C * r) * jax.nn.softplus(-lam_ref[...])
    a = jnp.exp(log_a)
    mult = jnp.sqrt(jnp.maximum(-_expm1(2.0 * log_a), 0.0))
    return a, mult * i * xc


def _group_scan(a, u, rows):
    for s in (1, 2, 4):
        keep = rows >= s
        a_sh = jnp.where(keep, pltpu.roll(a, s, 0), 1.0)
        u_sh = jnp.where(keep, pltpu.roll(u, s, 0), 0.0)
        u = a * u_sh + u
        a = a * a_sh
    return a, u


def _rec_prompt_body(x_ref, gate_ref, cw_ref, cb_ref, wai_ref, ba_ref, bi_ref, lam_ref, y_ref, ht_ref,
                     xbuf, a_s, u_s, h_s, *, tt):
    j = pl.program_id(1)
    w = x_ref.shape[1]

    @pl.when(j == 0)
    def _():
        xbuf[0:SUBLANES, :] = jnp.zeros((SUBLANES, w), F32)
        h_s[...] = jnp.zeros_like(h_s)

    xbuf[SUBLANES:SUBLANES + tt, :] = x_ref[...]
    xc = cb_ref[...]
    for k in range(CONV_W):
        lo = SUBLANES - (CONV_W - 1) + k
        xc = xc + cw_ref[k:k + 1, :] * xbuf[lo:lo + tt, :]
    xbuf[0:SUBLANES, :] = xbuf[tt:tt + SUBLANES, :]
    a, u = _lru_gates(xc, wai_ref, ba_ref, bi_ref, lam_ref)
    a_s[...] = a
    u_s[...] = u
    rows = lax.broadcasted_iota(I32, (SUBLANES, w), 0)

    def grp(g, h):
        r0 = pl.multiple_of(g * SUBLANES, SUBLANES)
        ag, ug = _group_scan(a_s[pl.ds(r0, SUBLANES), :], u_s[pl.ds(r0, SUBLANES), :], rows)
        hs = ug + ag * h
        y_ref[pl.ds(r0, SUBLANES), :] = jax.nn.gelu(gate_ref[pl.ds(r0, SUBLANES), :]) * hs
        return hs[SUBLANES - 1:SUBLANES, :]

    h = lax.fori_loop(0, tt // SUBLANES, grp, h_s[...])
    h_s[...] = h
    ht_ref[0] = h


def _rec_prompt(xrg, gate, n_seq, seq_len, conv_w, conv_b, wai_bf, b_a, b_i, lam):
    w = xrg.shape[1]
    tt = min(SCAN_TILE, seq_len)
    nt = seq_len // tt
    row = lambda b, j: (b * nt + j, 0)
    vec = lambda a: a.reshape(1, w)
    return pl.pallas_call(
        functools.partial(_rec_prompt_body, tt=tt),
        grid=(n_seq, nt),
        in_specs=[pl.BlockSpec((tt, w), row), pl.BlockSpec((tt, w), row), _const_spec((CONV_W, w)),
                  _const_spec((1, w)), _const_spec(wai_bf.shape), _const_spec((1, w)), _const_spec((1, w)),
                  _const_spec((1, w))],
        out_specs=[pl.BlockSpec((tt, w), row), pl.BlockSpec((1, 1, w), lambda b, j: (b, 0, 0))],
        out_shape=[jax.ShapeDtypeStruct((n_seq * seq_len, w), F32), jax.ShapeDtypeStruct((n_seq, 1, w), F32)],
        scratch_shapes=[pltpu.VMEM((tt + SUBLANES, w), F32), pltpu.VMEM((tt, w), F32), pltpu.VMEM((tt, w), F32),
                        pltpu.VMEM((1, w), F32)],
        compiler_params=_params(("parallel", "arbitrary")),
        name="rglru_prompt",
    )(xrg, gate, conv_w, vec(conv_b), wai_bf, vec(b_a), vec(b_i), vec(lam))


def _rec_sample_body(x0_ref, x1_ref, x2_ref, x3_ref, gate_ref, h0_ref, cw_ref, cb_ref, wai_ref, ba_ref, bi_ref,
                     lam_ref, y_ref, ht_ref, a_s, u_s):
    tm, w = gate_ref.shape
    xc = cb_ref[...]
    for k, x_ref in enumerate((x0_ref, x1_ref, x2_ref, x3_ref)):
        xc = xc + cw_ref[k:k + 1, :] * x_ref[...]
    a, u = _lru_gates(xc, wai_ref, ba_ref, bi_ref, lam_ref)
    a_s[...] = a
    u_s[...] = u
    rows = lax.broadcasted_iota(I32, (SUBLANES, w), 0)

    def grp(g, carry):
        r0 = pl.multiple_of(g * SUBLANES, SUBLANES)
        ag, ug = _group_scan(a_s[pl.ds(r0, SUBLANES), :], u_s[pl.ds(r0, SUBLANES), :], rows)
        hs = ug + ag * h0_ref[pl.ds(g, 1), :]
        y_ref[pl.ds(r0, SUBLANES), :] = jax.nn.gelu(gate_ref[pl.ds(r0, SUBLANES), :]) * hs
        ht_ref[pl.ds(g, 1), :] = hs[SUBLANES - 1:SUBLANES, :]
        return carry

    lax.fori_loop(0, tm // SUBLANES, grp, 0)


def _rec_sample(x_shift, gate, gate_row0, h0, conv_w, conv_b, wai_bf, b_a, b_i, lam):
    n, w = x_shift[0].shape
    tm = min(TOKEN_TILE, n)
    g8 = tm // SUBLANES
    g0 = gate_row0 // tm
    row = lambda i: (i, 0)
    vec = lambda a: a.reshape(1, w)
    return pl.pallas_call(
        _rec_sample_body,
        grid=(n // tm,),
        in_specs=[pl.BlockSpec((tm, w), row)] * CONV_W
        + [pl.BlockSpec((tm, w), lambda i: (g0 + i, 0)), pl.BlockSpec((g8, w), row), _const_spec((CONV_W, w)),
           _const_spec((1, w)), _const_spec(wai_bf.shape), _const_spec((1, w)), _const_spec((1, w)),
           _const_spec((1, w))],
        out_specs=[pl.BlockSpec((tm, w), row), pl.BlockSpec((g8, w), row)],
        out_shape=[jax.ShapeDtypeStruct((n, w), F32), jax.ShapeDtypeStruct((n // SUBLANES, w), F32)],
        scratch_shapes=[pltpu.VMEM((tm, w), F32), pltpu.VMEM((tm, w), F32)],
        compiler_params=_params(("parallel",)),
        name="rglru_sample",
    )(*x_shift, gate, h0, conv_w, vec(conv_b), wai_bf, vec(b_a), vec(b_i), vec(lam))


def _softmax_step(s, m, l, acc, cb):
    m_new = jnp.maximum(m, jnp.max(s, axis=-1, keepdims=True))
    corr = jnp.exp(m - m_new)
    p = jnp.exp(s - m_new)
    l = l * corr + jnp.sum(p, axis=-1, keepdims=True)
    acc = acc * corr + _dot(p.astype(BF16), cb)
    return m_new, l, acc


def _pattn_body(qa_ref, qp_ref, c_ref, kp_ref, kst_ref, o_ref, qa_s, qp_s, m_s, l_s, acc_s, *, tq, tk, scale):
    i = pl.program_id(1)
    j = pl.program_id(2)
    nh = qa_s.shape[0]
    kv = qa_s.shape[2]
    rope = qp_s.shape[2]

    @pl.when(j == 0)
    def _():
        for h in range(nh):
            qa_s[h] = qa_ref[:, h].reshape(tq, kv).astype(BF16)
            qp_s[h] = qp_ref[:, h].reshape(tq, rope).astype(BF16)
        m_s[...] = jnp.full_like(m_s, -jnp.inf)
        l_s[...] = jnp.zeros_like(l_s)
        acc_s[...] = jnp.zeros_like(acc_s)

    @pl.when(j * tk < (i + 1) * tq)
    def _():
        cb = c_ref[...].astype(BF16)
        kb = kp_ref[...].astype(BF16)
        qpos = i * tq + lax.broadcasted_iota(I32, (tq, tk), 0)
        kpos = j * tk + lax.broadcasted_iota(I32, (tq, tk), 1)
        visible = kpos <= qpos
        for h in range(nh):
            s = _dot_nt(qa_s[h], cb) + _dot_nt(qp_s[h], kb)
            s = (s * kst_ref[h:h + 1, :]) * scale
            s = jnp.where(visible, s, -jnp.inf)
            m_s[h], l_s[h], acc_s[h] = _softmax_step(s, m_s[h], l_s[h], acc_s[h], cb)

    @pl.when(j == pl.num_programs(2) - 1)
    def _():
        for h in range(nh):
            o_ref[:, h] = (acc_s[h] / l_s[h]).reshape(tq // SUBLANES, SUBLANES, kv)


def _prompt_attention(qabs, qpe, c, kpe, kst, n_seq, seq_len):
    _, nh, _, kv = qabs.shape
    rope = qpe.shape[3]
    tq = min(ATTN_TQ, seq_len)
    tk = min(ATTN_TK, seq_len)
    nq, nk = seq_len // tq, seq_len // tk
    g8 = tq // SUBLANES
    scale = 1.0 / math.sqrt(QK_NOPE + rope)

    def kblk(i, j):
        return jnp.minimum(j, ((i + 1) * tq - 1) // tk)

    return pl.pallas_call(
        functools.partial(_pattn_body, tq=tq, tk=tk, scale=scale),
        grid=(n_seq, nq, nk),
        in_specs=[pl.BlockSpec((g8, nh, SUBLANES, kv), lambda b, i, j: (b * nq + i, 0, 0, 0)),
                  pl.BlockSpec((g8, nh, SUBLANES, rope), lambda b, i, j: (b * nq + i, 0, 0, 0)),
                  pl.BlockSpec((tk, kv), lambda b, i, j: (b * nk + kblk(i, j), 0)),
                  pl.BlockSpec((tk, rope), lambda b, i, j: (b * nk + kblk(i, j), 0)),
                  pl.BlockSpec((nh, tk), lambda b, i, j: (0, b * nk + kblk(i, j)))],
        out_specs=pl.BlockSpec((g8, nh, SUBLANES, kv), lambda b, i, j: (b * nq + i, 0, 0, 0)),
        out_shape=jax.ShapeDtypeStruct((n_seq * seq_len // SUBLANES, nh, SUBLANES, kv), F32),
        scratch_shapes=[pltpu.VMEM((nh, tq, kv), BF16), pltpu.VMEM((nh, tq, rope), BF16),
                        pltpu.VMEM((nh, tq, 1), F32), pltpu.VMEM((nh, tq, 1), F32), pltpu.VMEM((nh, tq, kv), F32)],
        compiler_params=_params(("parallel", "parallel", "arbitrary")),
        name="mla_prompt_attention",
    )(qabs, qpe, c, kpe, kst)


def _expand_key_scale(ks, nh, t):
    rows = lax.broadcasted_iota(I32, (nh * t, nh), 0) // t
    head = lax.broadcasted_iota(I32, (nh * t, nh), 1)
    sel = jnp.where(rows == head, 1.0, 0.0).astype(BF16)
    hi = ks.astype(BF16)
    r1 = ks - hi.astype(F32)
    mid = r1.astype(BF16)
    lo = (r1 - mid.astype(F32)).astype(BF16)
    return (_dot_nt(sel, lo) + _dot_nt(sel, mid)) + _dot_nt(sel, hi)


def _sattn_body(pt_ref, qa_ref, qp_ref, cn_ref, kn_ref, sn_ref, ckv_hbm, kpe_hbm, ks_hbm, o_ref,
                cbuf, kbuf, sbuf, sem, *, n_pages, group, scale):
    b = pl.program_id(0)
    nb = pl.num_programs(0)
    _, nh, t, kv = qa_ref.shape
    rope = qp_ref.shape[3]
    page = cbuf.shape[2]
    ng = n_pages // group
    m_rows = nh * t

    def copies(bb, g, slot, p):
        phys = pt_ref[bb * n_pages + g * group + p]
        return (pltpu.make_async_copy(ckv_hbm.at[phys], cbuf.at[slot, p], sem.at[slot, 0]),
                pltpu.make_async_copy(kpe_hbm.at[phys], kbuf.at[slot, p], sem.at[slot, 1]),
                pltpu.make_async_copy(ks_hbm.at[phys], sbuf.at[slot, p], sem.at[slot, 2]))

    def start_group(bb, g, slot):
        for p in range(group):
            for cp in copies(bb, g, slot, p):
                cp.start()

    def wait_group(bb, g, slot):
        for p in range(group):
            for cp in copies(bb, g, slot, p):
                cp.wait()

    @pl.when(b == 0)
    def _():
        start_group(0, 0, 0)

    qa = qa_ref[0].reshape(m_rows, kv).astype(BF16)
    qp = qp_ref[0].reshape(m_rows, rope).astype(BF16)

    def attend(carry, c, kp, ks, visible):
        cb = c.astype(BF16)
        s = _dot_nt(qa, cb) + _dot_nt(qp, kp.astype(BF16))
        s = (s * _expand_key_scale(ks, nh, t)) * scale
        if visible is not None:
            s = jnp.where(visible, s, -jnp.inf)
        return _softmax_step(s, *carry, cb)

    def grp(g, carry):
        step = b * ng + g
        slot = step % 2

        @pl.when(g + 1 < ng)
        def _():
            start_group(b, g + 1, 1 - slot)

        @pl.when((g + 1 == ng) & (b + 1 < nb))
        def _():
            start_group(b + 1, 0, 1 - slot)

        wait_group(b, g, slot)
        n = group * page
        return attend(carry, cbuf[slot].reshape(n, kv), kbuf[slot].reshape(n, rope), sbuf[slot].reshape(n, nh), None)

    init = (jnp.full((m_rows, 1), -jnp.inf, F32), jnp.zeros((m_rows, 1), F32), jnp.zeros((m_rows, kv), F32))
    carry = lax.fori_loop(0, ng, grp, init)
    q_t = lax.broadcasted_iota(I32, (m_rows, t), 0) % t
    k_t = lax.broadcasted_iota(I32, (m_rows, t), 1)
    _, l, acc = attend(carry, cn_ref[...], kn_ref[...], sn_ref[...], k_t <= q_t)
    o_ref[0] = (acc / l).reshape(nh, t, kv)


def _sample_attention(qabs, qpe, c, kpe, ks, row0, n_seq, page_table, cache_ckv, cache_kpe, cache_kscale):
    _, nh, t, kv = qabs.shape
    rope = qpe.shape[3]
    n_pages = page_table.shape[1]
    page = cache_ckv.shape[1]
    group = math.gcd(PAGES_PER_GROUP, n_pages)
    g0 = row0 // t
    scale = 1.0 / math.sqrt(QK_NOPE + rope)
    any_spec = pl.BlockSpec(memory_space=pl.ANY)
    grid_spec = pltpu.PrefetchScalarGridSpec(
        num_scalar_prefetch=1,
        grid=(n_seq,),
        in_specs=[pl.BlockSpec((1, nh, t, kv), lambda b, pt: (g0 + b, 0, 0, 0)),
                  pl.BlockSpec((1, nh, t, rope), lambda b, pt: (g0 + b, 0, 0, 0)),
                  pl.BlockSpec((t, kv), lambda b, pt: (g0 + b, 0)),
                  pl.BlockSpec((t, rope), lambda b, pt: (g0 + b, 0)),
                  pl.BlockSpec((t, nh), lambda b, pt: (g0 + b, 0)),
                  any_spec, any_spec, any_spec],
        out_specs=pl.BlockSpec((1, nh, t, kv), lambda b, pt: (b, 0, 0, 0)),
        scratch_shapes=[pltpu.VMEM((2, group, page, kv), F32), pltpu.VMEM((2, group, page, rope), F32),
                        pltpu.VMEM((2, group, page, nh), F32), pltpu.SemaphoreType.DMA((2, 3))],
    )
    return pl.pallas_call(
        functools.partial(_sattn_body, n_pages=n_pages, group=group, scale=scale),
        grid_spec=grid_spec,
        out_shape=jax.ShapeDtypeStruct((n_seq, nh, t, kv), F32),
        compiler_params=_params(("arbitrary",)),
        name="mla_sample_attention",
    )(page_table.reshape(-1), qabs, qpe, c, kpe, ks, cache_ckv, cache_kpe, cache_kscale)


def _merge_body(x_ref, yrg_ref, ol_ref, wuv_ref, g1_ref, g2_ref, wout_ref, gf_ref, wr_ref, br_ref,
                h_ref, xn_ref, ti_ref, tg_ref):
    tm = x_ref.shape[0]
    _, nh, _, kv = ol_ref.shape
    w1 = yrg_ref.shape[1]
    n_exp = wr_ref.shape[1]
    o = jnp.concatenate([_dot(ol_ref[:, h].reshape(tm, kv).astype(BF16), wuv_ref[h]) for h in range(nh)], axis=1)
    y1 = (_rms(yrg_ref[...]) * g1_ref[...]).astype(BF16)
    y2 = (_rms(o) * g2_ref[...]).astype(BF16)
    hh = x_ref[...] + (_dot(y1, wout_ref[:w1, :]) + _dot(y2, wout_ref[w1:, :]))
    h_ref[...] = hh
    xn = _rms(hh) * gf_ref[...]
    xn_ref[...] = xn
    logits = _dot(xn.astype(BF16), wr_ref[...]) + br_ref[...]
    lane = lax.broadcasted_iota(I32, (tm, n_exp), 1)
    vals, idxs = [], []
    for _ in range(TOP_K):
        m = jnp.max(logits, axis=-1, keepdims=True)
        idx = jnp.min(jnp.where(logits == m, lane, n_exp), axis=-1, keepdims=True)
        vals.append(m)
        idxs.append(idx)
        logits = jnp.where(lane == idx, -jnp.inf, logits)
    es = [jnp.exp(v - vals[0]) for v in vals]
    den = es[0]
    for e in es[1:]:
        den = den + e
    lane_k = lax.broadcasted_iota(I32, (tm, TOP_K), 1)
    ti = jnp.zeros((tm, TOP_K), I32)
    tg = jnp.zeros((tm, TOP_K), F32)
    for k in range(TOP_K):
        ti = jnp.where(lane_k == k, idxs[k], ti)
        tg = jnp.where(lane_k == k, es[k] / den, tg)
    ti_ref[...] = ti
    tg_ref[...] = tg


def _merge_route(x, yrg, olat, wuv_bf, g1, g2, wout_bf, gf, wr_bf, b_router):
    n, d = x.shape
    w1 = yrg.shape[1]
    _, nh, _, kv = olat.shape
    w2 = wuv_bf.shape[0] * wuv_bf.shape[2]
    n_exp = wr_bf.shape[1]
    tm = min(TOKEN_TILE, n)
    row = lambda i: (i, 0)
    return pl.pallas_call(
        _merge_body,
        grid=(n // tm,),
        in_specs=[pl.BlockSpec((tm, d), row), pl.BlockSpec((tm, w1), row),
                  pl.BlockSpec((tm // SUBLANES, nh, SUBLANES, kv), lambda i: (i, 0, 0, 0)),
                  _const_spec(wuv_bf.shape), _const_spec((1, w1)), _const_spec((1, w2)), _const_spec(wout_bf.shape),
                  _const_spec((1, d)), _const_spec(wr_bf.shape), _const_spec((1, n_exp))],
        out_specs=[pl.BlockSpec((tm, d), row), pl.BlockSpec((tm, d), row), pl.BlockSpec((tm, TOP_K), row),
                   pl.BlockSpec((tm, TOP_K), row)],
        out_shape=[jax.ShapeDtypeStruct((n, d), F32), jax.ShapeDtypeStruct((n, d), F32),
                   jax.ShapeDtypeStruct((n, TOP_K), I32), jax.ShapeDtypeStruct((n, TOP_K), F32)],
        compiler_params=_params(("parallel",)),
        name="merge_route",
    )(x, yrg, olat, wuv_bf, g1.reshape(1, w1), g2.reshape(1, w2), wout_bf, gf.reshape(1, d), wr_bf,
      b_router.reshape(1, n_exp))


def _moe_plan(top_i, n_exp):
    n, k = top_i.shape
    n_sub = -(-n * k // MOE_SUB) + n_exp
    n_rows = n_sub * MOE_SUB
    n_wi = n_exp + n_sub // MOE_CAP_SUB
    flat = top_i.reshape(-1)
    onehot = (flat[:, None] == jnp.arange(n_exp, dtype=I32)[None, :]).astype(I32)
    csum = jnp.cumsum(onehot, axis=0)
    counts = csum[-1]
    rank = jnp.sum(onehot * csum, axis=1) - 1
    nsub_e = (counts + MOE_SUB - 1) // MOE_SUB
    sub_off = jnp.cumsum(nsub_e) - nsub_e
    pos = (sub_off * MOE_SUB)[flat] + rank
    row_token = jnp.zeros((n_rows,), I32).at[pos].set(jnp.arange(n * k, dtype=I32) // k, unique_indices=True)
    nwi_e = (nsub_e + MOE_CAP_SUB - 1) // MOE_CAP_SUB
    wi_end = jnp.cumsum(nwi_e)
    w = jnp.arange(n_wi, dtype=I32)
    e_w = jnp.minimum(jnp.searchsorted(wi_end, w, side="right").astype(I32), n_exp - 1)
    local = w - (wi_end - nwi_e)[e_w]
    valid = w < wi_end[-1]
    first = jnp.where(valid, sub_off[e_w] + local * MOE_CAP_SUB, 0)
    cnt = jnp.where(valid, jnp.clip(nsub_e[e_w] - local * MOE_CAP_SUB, 0, MOE_CAP_SUB), 0)
    last_e = e_w[jnp.maximum(wi_end[-1] - 1, 0)]
    e_w = jnp.where(valid, e_w, last_e)
    return n_rows, n_wi, pos.astype(I32), row_token, e_w.astype(I32), first.astype(I32), cnt.astype(I32)


def _dispatch_body(rt_ref, xn_hbm, o_ref, stage, sem):
    tile = pl.program_id(0)
    sub = stage.shape[0]
    base = tile * sub

    def row_copy(r, tok):
        return pltpu.make_async_copy(xn_hbm.at[pl.ds(tok, 1), :], stage.at[pl.ds(r, 1), :], sem.at[0])

    def issue(r, c):
        row_copy(r, rt_ref[base + r]).start()
        return c

    def drain(r, c):
        row_copy(r, 0).wait()
        return c

    lax.fori_loop(0, sub, issue, 0)
    lax.fori_loop(0, sub, drain, 0)
    o_ref[...] = stage[...].astype(o_ref.dtype)


def _dispatch(xn, row_token, n_rows):
    n, d = xn.shape
    grid_spec = pltpu.PrefetchScalarGridSpec(
        num_scalar_prefetch=1,
        grid=(n_rows // MOE_SUB,),
        in_specs=[pl.BlockSpec(memory_space=pl.ANY)],
        out_specs=pl.BlockSpec((MOE_SUB, d), lambda i, rt: (i, 0)),
        scratch_shapes=[pltpu.VMEM((MOE_SUB, d), xn.dtype), pltpu.SemaphoreType.DMA((1,))],
    )
    return pl.pallas_call(
        _dispatch_body,
        grid_spec=grid_spec,
        out_shape=jax.ShapeDtypeStruct((n_rows, d), BF16),
        compiler_params=_params(("arbitrary",)),
        name="moe_dispatch",
    )(row_token, xn)


def _expert_body(e_ref, first_ref, cnt_ref, xg_hbm, wg_ref, bg_ref, wu_ref, bu_ref, wd_ref, bd_ref, out_hbm,
                 xbuf, acc, sem):
    w = pl.program_id(0)
    f = pl.program_id(1)
    nf = pl.num_programs(1)
    first = first_ref[w]
    cnt = cnt_ref[w]
    sub = MOE_SUB

    def rows_in(s):
        return pltpu.make_async_copy(xg_hbm.at[pl.ds((first + s) * sub, sub), :],
                                     xbuf.at[pl.ds(s * sub, sub), :], sem.at[0])

    def rows_out(s):
        return pltpu.make_async_copy(acc.at[pl.ds(s * sub, sub), :],
                                     out_hbm.at[pl.ds((first + s) * sub, sub), :], sem.at[1])

    def each(fn):
        def body(s, c):
            fn(s)
            return c
        lax.fori_loop(0, cnt, body, 0)

    @pl.when(f == 0)
    def _():
        each(lambda s: rows_in(s).start())
        each(lambda s: rows_in(s).wait())

    @pl.when(cnt > 0)
    def _():
        wg = wg_ref[0].astype(BF16)
        wu = wu_ref[0].astype(BF16)
        wd = wd_ref[0].astype(BF16)
        bg = bg_ref[0]
        bu = bu_ref[0]

        def sub_tile(s):
            r0 = pl.multiple_of(s * sub, sub)
            xs = xbuf[pl.ds(r0, sub), :]
            g = jnp.minimum(_dot(xs, wg) + bg, SWIGLU_LIMIT)
            u = jnp.clip(_dot(xs, wu) + bu, -SWIGLU_LIMIT, SWIGLU_LIMIT)
            act = (u + 1.0) * g * jax.nn.sigmoid(SWIGLU_ALPHA * g)
            part = _dot(act.astype(BF16), wd)

            @pl.when(f == 0)
            def _():
                acc[pl.ds(r0, sub), :] = part + bd_ref[0]

            @pl.when(f > 0)
            def _():
                acc[pl.ds(r0, sub), :] += part

        each(sub_tile)

    @pl.when(f == nf - 1)
    def _():
        each(lambda s: rows_out(s).start())
        each(lambda s: rows_out(s).wait())


def _experts(xg, e_w, first, cnt, w_gate, b_gate, w_up, b_up, w_down, b_down):
    n_rows, d = xg.shape
    n_exp, _, de = w_gate.shape
    n_wi = e_w.shape[0]
    fch = min(MOE_F, de)
    cap = MOE_SUB * MOE_CAP_SUB
    nf = de // fch

    def fblk(w, f, c):
        return jnp.where(c[w] > 0, f, nf - 1)

    grid_spec = pltpu.PrefetchScalarGridSpec(
        num_scalar_prefetch=3,
        grid=(n_wi, nf),
        in_specs=[pl.BlockSpec(memory_space=pl.ANY),
                  pl.BlockSpec((1, d, fch), lambda w, f, e, a, c: (e[w], 0, fblk(w, f, c))),
                  pl.BlockSpec((1, 1, fch), lambda w, f, e, a, c: (e[w], 0, fblk(w, f, c))),
                  pl.BlockSpec((1, d, fch), lambda w, f, e, a, c: (e[w], 0, fblk(w, f, c))),
                  pl.BlockSpec((1, 1, fch), lambda w, f, e, a, c: (e[w], 0, fblk(w, f, c))),
                  pl.BlockSpec((1, fch, d), lambda w, f, e, a, c: (e[w], fblk(w, f, c), 0)),
                  pl.BlockSpec((1, 1, d), lambda w, f, e, a, c: (e[w], 0, 0))],
        out_specs=pl.BlockSpec(memory_space=pl.ANY),
        scratch_shapes=[pltpu.VMEM((cap, d), BF16), pltpu.VMEM((cap, d), F32), pltpu.SemaphoreType.DMA((2,))],
    )
    return pl.pallas_call(
        _expert_body,
        grid_spec=grid_spec,
        out_shape=jax.ShapeDtypeStruct((n_rows, d), F32),
        compiler_params=_params(("arbitrary", "arbitrary")),
        name="moe_experts",
    )(e_w, first, cnt, xg, w_gate, b_gate.reshape(n_exp, 1, de), w_up, b_up.reshape(n_exp, 1, de), w_down,
      b_down.reshape(n_exp, 1, d))


def _combine_body(pos_ref, h_ref, tg_ref, eo_hbm, y_ref, rows, sem):
    tile = pl.program_id(0)
    tm = h_ref.shape[0]
    base = tile * tm

    def row_copy(n, k, r):
        return pltpu.make_async_copy(eo_hbm.at[pl.ds(r, 1), :], rows.at[k, pl.ds(n, 1), :], sem.at[0])

    def issue(n, c):
        for k in range(TOP_K):
            row_copy(n, k, pos_ref[(base + n) * TOP_K + k]).start()
        return c

    def drain(n, c):
        for k in range(TOP_K):
            row_copy(n, k, 0).wait()
        return c

    lax.fori_loop(0, tm, issue, 0)
    lax.fori_loop(0, tm, drain, 0)
    tg = tg_ref[...]
    out = tg[:, 0:1] * rows[0]
    for k in range(1, TOP_K):
        out = out + tg[:, k:k + 1] * rows[k]
    y_ref[...] = h_ref[...] + out


def _combine(h, tg, pos, eo):
    n, d = h.shape
    tm = min(COMBINE_TILE, n)
    grid_spec = pltpu.PrefetchScalarGridSpec(
        num_scalar_prefetch=1,
        grid=(n // tm,),
        in_specs=[pl.BlockSpec((tm, d), lambda i, p: (i, 0)), pl.BlockSpec((tm, TOP_K), lambda i, p: (i, 0)),
                  pl.BlockSpec(memory_space=pl.ANY)],
        out_specs=pl.BlockSpec((tm, d), lambda i, p: (i, 0)),
        scratch_shapes=[pltpu.VMEM((TOP_K, tm, d), F32), pltpu.SemaphoreType.DMA((1,))],
    )
    return pl.pallas_call(
        _combine_body,
        grid_spec=grid_spec,
        out_shape=jax.ShapeDtypeStruct((n, d), F32),
        compiler_params=_params(("arbitrary",)),
        name="moe_combine",
    )(pos, h, tg, eo)


def _moe(h, xn, ti, tg, w_gate, b_gate, w_up, b_up, w_down, b_down):
    n_exp = w_gate.shape[0]
    n_rows, _, pos, row_token, e_w, first, cnt = _moe_plan(ti, n_exp)
    xg = _dispatch(xn, row_token, n_rows)
    eo = _experts(xg, e_w, first, cnt, w_gate, b_gate, w_up, b_up, w_down, b_down)
    return _combine(h, tg, pos, eo)


def _rope_tables(pos, rope, reps):
    half = rope // 2
    inv_freq = jnp.power(ROPE_THETA, -jnp.arange(half, dtype=F32) / half)
    ang = pos.astype(F32)[:, None] * inv_freq[None, :]
    cos, sin = jnp.cos(ang), jnp.sin(ang)
    return (jnp.tile(jnp.concatenate([cos, cos], axis=1), (1, reps)),
            jnp.tile(jnp.concatenate([-sin, sin], axis=1), (1, reps)))


def kernel(x_prompt, x_sample, cache_ckv, cache_kpe, cache_kscale, state_conv, state_h, page_table,
           norm_attn_gain, w_in, conv_w, conv_b, lru_w_a, lru_b_a, lru_w_i, lru_b_i, lru_lambda,
           q_norm_gain, w_uq, kv_norm_gain, w_uk, w_uv, qk_gain_q, qk_gain_k,
           out_norm_rg, out_norm_mla, w_out, norm_ffn_gain, w_router, b_router,
           w_gate, b_gate, w_up, b_up, w_down, b_down):
    nb, seq, d = x_prompt.shape
    db, t, _ = x_sample.shape
    assert t == SUBLANES, "the sample group is laid out one sequence per sublane group"
    lru_w = state_h.shape[1]
    kv_lora, nh, _ = w_uk.shape
    q_lora = w_uq.shape[0]
    rope = cache_kpe.shape[2]
    qk_head = QK_NOPE + rope
    n_p, n_s = nb * seq, db * t
    past_len = page_table.shape[1] * cache_ckv.shape[1]

    x_all = jnp.concatenate([x_prompt.reshape(n_p, d), x_sample.reshape(n_s, d)], axis=0)
    pos = jnp.concatenate([jnp.tile(jnp.arange(seq), nb), jnp.tile(past_len + jnp.arange(t), db)])
    cos_w, sin_w = _rope_tables(pos, rope, nh)

    xrg, gate, ql, kvl, kr = _in_proj(x_all, norm_attn_gain, w_in.astype(BF16),
                                      (lru_w, lru_w, q_lora, kv_lora, rope))

    w_uq3 = w_uq.reshape(q_lora, nh, qk_head)
    wqn_bf = w_uq3[:, :, :QK_NOPE].reshape(q_lora, nh * QK_NOPE).astype(BF16)
    wqr_bf = w_uq3[:, :, QK_NOPE:].reshape(q_lora, nh * rope).astype(BF16)
    wukt_bf = jnp.transpose(w_uk, (1, 2, 0)).astype(BF16)
    wuk_bf = w_uk.reshape(kv_lora, nh * QK_NOPE).astype(BF16)
    qabs, qpe, c, kpe, ks = _mla_proj(
        ql, kvl, kr, cos_w, sin_w, q_norm_gain, wqn_bf, wqr_bf, qk_gain_q[:QK_NOPE],
        jnp.tile(qk_gain_q[QK_NOPE:], nh), qk_gain_k[:QK_NOPE], qk_gain_k[QK_NOPE:], kv_norm_gain,
        wukt_bf, wuk_bf, nh)

    wai_bf = jnp.concatenate([lru_w_a, lru_w_i], axis=-1).astype(BF16)
    y_p, h_p = _rec_prompt(xrg, gate, nb, seq, conv_w, conv_b, wai_bf, lru_b_a, lru_b_i, lru_lambda)
    xrg_s = xrg[n_p:].reshape(db, t, lru_w)
    xpad = jnp.concatenate([state_conv, xrg_s], axis=1)
    x_shift = [xpad[:, k:k + t].reshape(n_s, lru_w) for k in range(CONV_W)]
    y_s, h_s = _rec_sample(x_shift, gate, n_p, state_h, conv_w, conv_b, wai_bf, lru_b_a, lru_b_i, lru_lambda)
    p_conv = xrg[:n_p].reshape(nb, seq, lru_w)[:, seq - (CONV_W - 1):]
    s_conv = xpad[:, t:]

    kst = jnp.transpose(ks[:n_p])
    o_p = _prompt_attention(qabs, qpe, c, kpe, kst, nb, seq)
    o_s = _sample_attention(qabs, qpe, c, kpe, ks, n_p, db, page_table, cache_ckv, cache_kpe, cache_kscale)

    wuv_bf = jnp.transpose(w_uv, (1, 0, 2)).astype(BF16)
    hh, xn, ti, tg = _merge_route(x_all, jnp.concatenate([y_p, y_s], axis=0), jnp.concatenate([o_p, o_s], axis=0),
                                  wuv_bf, out_norm_rg, out_norm_mla, w_out.astype(BF16), norm_ffn_gain,
                                  w_router.astype(BF16), b_router)
    y_all = _moe(hh, xn, ti, tg, w_gate, b_gate, w_up, b_up, w_down, b_down)

    return (y_all[:n_p].reshape(nb, seq, d), y_all[n_p:].reshape(db, t, d),
            c[:n_p].reshape(nb, seq, kv_lora), kpe[:n_p].reshape(nb, seq, rope), ks[:n_p].reshape(nb, seq, nh),
            p_conv, h_p.reshape(nb, lru_w),
            c[n_p:].reshape(db, t, kv_lora), kpe[n_p:].reshape(db, t, rope), ks[n_p:].reshape(db, t, nh),
            s_conv, h_s)
```

```python
import functools
import math

import jax
import jax.numpy as jnp
from jax import lax
from jax.experimental import pallas as pl
from jax.experimental.pallas import tpu as pltpu

F32 = jnp.float32
BF16 = jnp.bfloat16
I32 = jnp.int32

EPS = 1e-6
LRU_C = 8.0
LRU_BLOCK_W = 128
CONV_W = 4
ROPE_THETA = 10000.0
QK_NOPE = 128
TOP_K = 4
SWIGLU_LIMIT = 7.0
SWIGLU_ALPHA = 1.702

SUBLANES = 8
LANES = 128
VMEM_LIMIT_BYTES = 56 * 1024 * 1024

TOKEN_TILE = 256
SCAN_TILE = 256
ATTN_TQ = 256
ATTN_TK = 512
PAGES_PER_GROUP = 16
PAGES_PER_SOFTMAX = 16
MOE_SUB = 256
MOE_CAP_SUB = 6
MOE_BLOCK_SUBS = 4
MOE_F = 256
MOE_DOWN_COLS = 512
DMA_UNROLL = 8
COMBINE_TILE = 128


def _params(sem, vmem=VMEM_LIMIT_BYTES):
    return pltpu.CompilerParams(dimension_semantics=sem, vmem_limit_bytes=vmem)


def _dot(a, b):
    return jnp.dot(a, b, preferred_element_type=F32)


def _dot_nt(a, b):
    return lax.dot_general(a, b, (((1,), (1,)), ((), ())), preferred_element_type=F32)


def _rms(x):
    return x * lax.rsqrt(jnp.mean(x * x, axis=-1, keepdims=True) + EPS)


def _const_spec(shape):
    nd = len(shape)
    return pl.BlockSpec(shape, lambda *_: (0,) * nd)


def _inproj_body(x_ref, g_ref, w_ref, *out_refs, offs):
    xn = (_rms(x_ref[...]) * g_ref[...]).astype(BF16)
    for o_ref, (a, b) in zip(out_refs, offs):
        o_ref[...] = _dot(xn, w_ref[:, a:b])


def _in_proj(x, gain, w_bf, widths):
    n, d = x.shape
    tm = min(TOKEN_TILE, n)
    offs, a = [], 0
    for w in widths:
        offs.append((a, a + w))
        a += w
    return pl.pallas_call(
        functools.partial(_inproj_body, offs=tuple(offs)),
        grid=(n // tm,),
        in_specs=[pl.BlockSpec((tm, d), lambda i: (i, 0)), _const_spec((1, d)), _const_spec(w_bf.shape)],
        out_specs=[pl.BlockSpec((tm, w), lambda i: (i, 0)) for w in widths],
        out_shape=[jax.ShapeDtypeStruct((n, w), F32) for w in widths],
        compiler_params=_params(("parallel",)),
        name="in_proj",
    )(x, gain.reshape(1, d), w_bf)


def _mla_proj_body(ql_ref, kvl_ref, kr_ref, cos_ref, sin_ref, qng_ref, wqn_ref, wqr_ref, gqn_ref, gqr_ref,
                   gkn_ref, gkr_ref, kvg_ref, wukt_ref, wuk_ref,
                   qabs_ref, qpe_ref, c_ref, kpe_ref, ks_ref, *, nh):
    tm = ql_ref.shape[0]
    rope = kr_ref.shape[1]
    half = rope // 2
    qk_head = QK_NOPE + rope
    g8 = tm // SUBLANES

    qn = (_rms(ql_ref[...]) * qng_ref[...]).astype(BF16)
    qnope = _dot(qn, wqn_ref[...])
    qrope = _dot(qn, wqr_ref[...])
    r2 = qrope * qrope
    lane = lax.broadcasted_iota(I32, (tm, LANES), 1)
    heads_per_vreg = LANES // rope
    inv_heads = []
    for h in range(nh):
        blk = qnope[:, h * QK_NOPE:(h + 1) * QK_NOPE]
        ssn = jnp.sum(blk * blk, axis=-1, keepdims=True)
        v, sub = divmod(h, heads_per_vreg)
        rblk = r2[:, v * LANES:(v + 1) * LANES]
        in_head = (lane >= sub * rope) & (lane < (sub + 1) * rope)
        ssr = jnp.sum(jnp.where(in_head, rblk, 0.0), axis=-1, keepdims=True)
        inv_h = lax.rsqrt((ssn + ssr) / qk_head + EPS)
        inv_heads.append(inv_h)
        qa_in = ((blk * inv_h) * gqn_ref[...]) * gkn_ref[...]
        qabs = _dot(qa_in.astype(BF16), wukt_ref[h])
        qabs_ref[:, h] = qabs.reshape(g8, SUBLANES, qabs.shape[1])

    cols = []
    for v in range(nh // heads_per_vreg):
        col = inv_heads[v * heads_per_vreg]
        for sub in range(1, heads_per_vreg):
            col = jnp.where(lane >= sub * rope, inv_heads[v * heads_per_vreg + sub], col)
        cols.append(col)
    inv_all = jnp.concatenate(cols, axis=1)
    qr = (qrope * inv_all) * gqr_ref[...]
    wide = lax.broadcasted_iota(I32, qr.shape, 1)
    first_half = (wide % rope) < half
    width = qr.shape[1]
    rot = jnp.where(first_half, pltpu.roll(qr, width - half, 1), pltpu.roll(qr, half, 1))
    qpe = qr * cos_ref[...] + rot * sin_ref[...]
    for h in range(nh):
        qpe_ref[:, h] = qpe[:, h * rope:(h + 1) * rope].reshape(g8, SUBLANES, rope)

    c = _rms(kvl_ref[...]) * kvg_ref[...]
    c_ref[...] = c
    kn = _dot(c.astype(BF16), wuk_ref[...])
    kr = kr_ref[...]
    sskr = jnp.sum(kr * kr, axis=-1, keepdims=True)
    lane_h = lax.broadcasted_iota(I32, (tm, nh), 1)
    ks = jnp.zeros((tm, nh), F32)
    for h in range(nh):
        blk = kn[:, h * QK_NOPE:(h + 1) * QK_NOPE]
        ssk = jnp.sum(blk * blk, axis=-1, keepdims=True)
        ks = jnp.where(lane_h == h, lax.rsqrt((ssk + sskr) / qk_head + EPS), ks)
    ks_ref[...] = ks
    krg = kr * gkr_ref[...]
    swapped = jnp.concatenate([krg[:, half:], krg[:, :half]], axis=1)
    kpe_ref[...] = krg * cos_ref[:, :rope] + swapped * sin_ref[:, :rope]


def _mla_proj(ql, kvl, kr, cos_w, sin_w, q_norm_gain, wqn_bf, wqr_bf, gqn, gqr_w, gkn, gkr, kv_norm_gain,
              wukt_bf, wuk_bf, nh):
    n, q_lora = ql.shape
    kv_lora = kvl.shape[1]
    rope = kr.shape[1]
    tm = min(TOKEN_TILE, n)
    g8 = tm // SUBLANES
    row = lambda i: (i, 0)
    row4 = lambda i: (i, 0, 0, 0)
    return pl.pallas_call(
        functools.partial(_mla_proj_body, nh=nh),
        grid=(n // tm,),
        in_specs=[pl.BlockSpec((tm, q_lora), row), pl.BlockSpec((tm, kv_lora), row), pl.BlockSpec((tm, rope), row),
                  pl.BlockSpec((tm, nh * rope), row), pl.BlockSpec((tm, nh * rope), row),
                  _const_spec((1, q_lora)), _const_spec(wqn_bf.shape), _const_spec(wqr_bf.shape),
                  _const_spec((1, QK_NOPE)), _const_spec((1, nh * rope)), _const_spec((1, QK_NOPE)),
                  _const_spec((1, rope)), _const_spec((1, kv_lora)), _const_spec(wukt_bf.shape),
                  _const_spec(wuk_bf.shape)],
        out_specs=[pl.BlockSpec((g8, nh, SUBLANES, kv_lora), row4), pl.BlockSpec((g8, nh, SUBLANES, rope), row4),
                   pl.BlockSpec((tm, kv_lora), row), pl.BlockSpec((tm, rope), row), pl.BlockSpec((tm, nh), row)],
        out_shape=[jax.ShapeDtypeStruct((n // SUBLANES, nh, SUBLANES, kv_lora), F32),
                   jax.ShapeDtypeStruct((n // SUBLANES, nh, SUBLANES, rope), F32),
                   jax.ShapeDtypeStruct((n, kv_lora), F32), jax.ShapeDtypeStruct((n, rope), F32),
                   jax.ShapeDtypeStruct((n, nh), F32)],
        compiler_params=_params(("parallel",)),
        name="mla_proj",
    )(ql, kvl, kr, cos_w, sin_w, q_norm_gain.reshape(1, -1), wqn_bf, wqr_bf, gqn.reshape(1, -1),
      gqr_w.reshape(1, -1), gkn.reshape(1, -1), gkr.reshape(1, -1), kv_norm_gain.reshape(1, -1), wukt_bf, wuk_bf)


def _expm1(x):
    u = jnp.exp(x)
    um1 = u - 1.0
    return jnp.where(u == 1.0, x, jnp.where(um1 == -1.0, -1.0, um1 * x / jnp.log(u)))


def _lru_gates(xc, wai_ref, ba_ref, bi_ref, lam_ref):
    nblk = wai_ref.shape[0]
    xcb = xc.astype(BF16)
    ra, ri = [], []
    for n in range(nblk):
        g = _dot(xcb[:, n * LRU_BLOCK_W:(n + 1) * LRU_BLOCK_W], wai_ref[n])
        ra.append(g[:, :LRU_BLOCK_W])
        ri.append(g[:, LRU_BLOCK_W:])
    r = jax.nn.sigmoid(jnp.concatenate(ra, axis=1) + ba_ref[...])
    i = jax.nn.sigmoid(jnp.concatenate(ri, axis=1) + bi_ref[...])
    log_a = (-LRU_C * r) * jax.nn.softplus(-lam_ref[...])
    a = jnp.exp(log_a)
    mult = jnp.sqrt(jnp.maximum(-_expm1(2.0 * log_a), 0.0))
    return a, mult * i * xc


def _group_scan(a, u, rows):
    for s in (1, 2, 4):
        keep = rows >= s
        a_sh = jnp.where(keep, pltpu.roll(a, s, 0), 1.0)
        u_sh = jnp.where(keep, pltpu.roll(u, s, 0), 0.0)
        u = a * u_sh + u
        a = a * a_sh
    return a, u


def _rec_prompt_body(x_ref, gate_ref, cw_ref, cb_ref, wai_ref, ba_ref, bi_ref, lam_ref, y_ref, ht_ref,
                     xbuf, a_s, u_s, h_s, *, tt):
    j = pl.program_id(1)
    w = x_ref.shape[1]

    @pl.when(j == 0)
    def _():
        xbuf[0:SUBLANES, :] = jnp.zeros((SUBLANES, w), F32)
        h_s[...] = jnp.zeros_like(h_s)

    xbuf[SUBLANES:SUBLANES + tt, :] = x_ref[...]
    xc = cb_ref[...]
    for k in range(CONV_W):
        lo = SUBLANES - (CONV_W - 1) + k
        xc = xc + cw_ref[k:k + 1, :] * xbuf[lo:lo + tt, :]
    xbuf[0:SUBLANES, :] = xbuf[tt:tt + SUBLANES, :]
    a, u = _lru_gates(xc, wai_ref, ba_ref, bi_ref, lam_ref)
    a_s[...] = a
    u_s[...] = u
    rows = lax.broadcasted_iota(I32, (SUBLANES, w), 0)

    def grp(g, h):
        r0 = pl.multiple_of(g * SUBLANES, SUBLANES)
        ag, ug = _group_scan(a_s[pl.ds(r0, SUBLANES), :], u_s[pl.ds(r0, SUBLANES), :], rows)
        hs = ug + ag * h
        y_ref[pl.ds(r0, SUBLANES), :] = jax.nn.gelu(gate_ref[pl.ds(r0, SUBLANES), :]) * hs
        return hs[SUBLANES - 1:SUBLANES, :]

    h = lax.fori_loop(0, tt // SUBLANES, grp, h_s[...])
    h_s[...] = h
    ht_ref[0] = h


def _rec_prompt(xrg, gate, n_seq, seq_len, conv_w, conv_b, wai_bf, b_a, b_i, lam):
    w = xrg.shape[1]
    tt = min(SCAN_TILE, seq_len)
    nt = seq_len // tt
    row = lambda b, j: (b * nt + j, 0)
    vec = lambda a: a.reshape(1, w)
    return pl.pallas_call(
        functools.partial(_rec_prompt_body, tt=tt),
        grid=(n_seq, nt),
        in_specs=[pl.BlockSpec((tt, w), row), pl.BlockSpec((tt, w), row), _const_spec((CONV_W, w)),
                  _const_spec((1, w)), _const_spec(wai_bf.shape), _const_spec((1, w)), _const_spec((1, w)),
                  _const_spec((1, w))],
        out_specs=[pl.BlockSpec((tt, w), row), pl.BlockSpec((1, 1, w), lambda b, j: (b, 0, 0))],
        out_shape=[jax.ShapeDtypeStruct((n_seq * seq_len, w), F32), jax.ShapeDtypeStruct((n_seq, 1, w), F32)],
        scratch_shapes=[pltpu.VMEM((tt + SUBLANES, w), F32), pltpu.VMEM((tt, w), F32), pltpu.VMEM((tt, w), F32),
                        pltpu.VMEM((1, w), F32)],
        compiler_params=_params(("parallel", "arbitrary")),
        name="rglru_prompt",
    )(xrg, gate, conv_w, vec(conv_b), wai_bf, vec(b_a), vec(b_i), vec(lam))


def _rec_sample_body(x0_ref, x1_ref, x2_ref, x3_ref, gate_ref, h0_ref, cw_ref, cb_ref, wai_ref, ba_ref, bi_ref,
                     lam_ref, y_ref, ht_ref, a_s, u_s):
    tm, w = gate_ref.shape
    xc = cb_ref[...]
    for k, x_ref in enumerate((x0_ref, x1_ref, x2_ref, x3_ref)):
        xc = xc + cw_ref[k:k + 1, :] * x_ref[...]
    a, u = _lru_gates(xc, wai_ref, ba_ref, bi_ref, lam_ref)
    a_s[...] = a
    u_s[...] = u
    rows = lax.broadcasted_iota(I32, (SUBLANES, w), 0)

    def grp(g, carry):
        r0 = pl.multiple_of(g * SUBLANES, SUBLANES)
        ag, ug = _group_scan(a_s[pl.ds(r0, SUBLANES), :], u_s[pl.ds(r0, SUBLANES), :], rows)
        hs = ug + ag * h0_ref[pl.ds(g, 1), :]
        y_ref[pl.ds(r0, SUBLANES), :] = jax.nn.gelu(gate_ref[pl.ds(r0, SUBLANES), :]) * hs
        ht_ref[pl.ds(g, 1), :] = hs[SUBLANES - 1:SUBLANES, :]
        return carry

    lax.fori_loop(0, tm // SUBLANES, grp, 0)


def _rec_sample(x_shift, gate, gate_row0, h0, conv_w, conv_b, wai_bf, b_a, b_i, lam):
    n, w = x_shift[0].shape
    tm = min(TOKEN_TILE, n)
    g8 = tm // SUBLANES
    g0 = gate_row0 // tm
    row = lambda i: (i, 0)
    vec = lambda a: a.reshape(1, w)
    return pl.pallas_call(
        _rec_sample_body,
        grid=(n // tm,),
        in_specs=[pl.BlockSpec((tm, w), row)] * CONV_W
        + [pl.BlockSpec((tm, w), lambda i: (g0 + i, 0)), pl.BlockSpec((g8, w), row), _const_spec((CONV_W, w)),
           _const_spec((1, w)), _const_spec(wai_bf.shape), _const_spec((1, w)), _const_spec((1, w)),
           _const_spec((1, w))],
        out_specs=[pl.BlockSpec((tm, w), row), pl.BlockSpec((g8, w), row)],
        out_shape=[jax.ShapeDtypeStruct((n, w), F32), jax.ShapeDtypeStruct((n // SUBLANES, w), F32)],
        scratch_shapes=[pltpu.VMEM((tm, w), F32), pltpu.VMEM((tm, w), F32)],
        compiler_params=_params(("parallel",)),
        name="rglru_sample",
    )(*x_shift, gate, h0, conv_w, vec(conv_b), wai_bf, vec(b_a), vec(b_i), vec(lam))


def _fold_lanes(x, op):
    n = x.shape[1]
    if n % LANES != 0 or n == LANES:
        return x
    out = x[:, :LANES]
    for c0 in range(LANES, n, LANES):
        out = op(out, x[:, c0:c0 + LANES])
    return out


def _softmax_step(s, m, l, acc, cb):
    m_new = jnp.maximum(m, jnp.max(_fold_lanes(s, jnp.maximum), axis=-1, keepdims=True))
    corr = jnp.exp2(m - m_new)
    p = jnp.exp2(s - m_new)
    l = l * corr + jnp.sum(_fold_lanes(p, jnp.add), axis=-1, keepdims=True)
    acc = acc * corr + _dot(p.astype(BF16), cb)
    return m_new, l, acc


def _pattn_body(qa_ref, qp_ref, c_ref, kp_ref, kst_ref, o_ref, qa_s, qp_s, m_s, l_s, acc_s, *, tq, tk, scale):
    i = pl.program_id(1)
    j = pl.program_id(2)
    nh = qa_s.shape[0]
    kv = qa_s.shape[2]
    rope = qp_s.shape[2]

    @pl.when(j == 0)
    def _():
        for h in range(nh):
            qa_s[h] = qa_ref[:, h].reshape(tq, kv).astype(BF16)
            qp_s[h] = qp_ref[:, h].reshape(tq, rope).astype(BF16)
        m_s[...] = jnp.full_like(m_s, -jnp.inf)
        l_s[...] = jnp.zeros_like(l_s)
        acc_s[...] = jnp.zeros_like(acc_s)

    def kv_tile(masked):
        cb = c_ref[...].astype(BF16)
        kb = kp_ref[...].astype(BF16)
        key_scale = kst_ref[...] * scale
        if masked:
            qpos = i * tq + lax.broadcasted_iota(I32, (tq, tk), 0)
            kpos = j * tk + lax.broadcasted_iota(I32, (tq, tk), 1)
            visible = kpos <= qpos

        def logits(h):
            return _dot_nt(qa_s[h], cb) + _dot_nt(qp_s[h], kb)

        qk_next = logits(0)
        for h in range(nh):
            s = qk_next * key_scale[h:h + 1, :]
            if h + 1 < nh:
                qk_next = logits(h + 1)
            if masked:
                s = jnp.where(visible, s, -jnp.inf)
            m_s[h], l_s[h], acc_s[h] = _softmax_step(s, m_s[h], l_s[h], acc_s[h], cb)

    active = j * tk < (i + 1) * tq
    crosses_diagonal = (j + 1) * tk - 1 > i * tq

    @pl.when(active & crosses_diagonal)
    def _():
        kv_tile(True)

    @pl.when(active & jnp.logical_not(crosses_diagonal))
    def _():
        kv_tile(False)

    @pl.when(j == pl.num_programs(2) - 1)
    def _():
        for h in range(nh):
            o_ref[:, h] = (acc_s[h] / l_s[h]).reshape(tq // SUBLANES, SUBLANES, kv)


def _prompt_attention(qabs, qpe, c, kpe, kst, n_seq, seq_len):
    n_groups, nh, _, kv = qabs.shape
    rope = qpe.shape[3]
    tq = min(ATTN_TQ, seq_len)
    tk = min(ATTN_TK, seq_len)
    nq, nk = seq_len // tq, seq_len // tk
    g8 = tq // SUBLANES
    scale = math.log2(math.e) / math.sqrt(QK_NOPE + rope)

    def kblk(i, j):
        return jnp.minimum(j, ((i + 1) * tq - 1) // tk)

    return pl.pallas_call(
        functools.partial(_pattn_body, tq=tq, tk=tk, scale=scale),
        grid=(n_seq, nq, nk),
        in_specs=[pl.BlockSpec((g8, nh, SUBLANES, kv), lambda b, i, j: (b * nq + i, 0, 0, 0)),
                  pl.BlockSpec((g8, nh, SUBLANES, rope), lambda b, i, j: (b * nq + i, 0, 0, 0)),
                  pl.BlockSpec((tk, kv), lambda b, i, j: (b * nk + kblk(i, j), 0)),
                  pl.BlockSpec((tk, rope), lambda b, i, j: (b * nk + kblk(i, j), 0)),
                  pl.BlockSpec((nh, tk), lambda b, i, j: (0, b * nk + kblk(i, j)))],
        out_specs=pl.BlockSpec((g8, nh, SUBLANES, kv), lambda b, i, j: (b * nq + i, 0, 0, 0)),
        out_shape=jax.ShapeDtypeStruct((n_groups, nh, SUBLANES, kv), F32),
        scratch_shapes=[pltpu.VMEM((nh, tq, kv), BF16), pltpu.VMEM((nh, tq, rope), BF16),
                        pltpu.VMEM((nh, tq, 1), F32), pltpu.VMEM((nh, tq, 1), F32), pltpu.VMEM((nh, tq, kv), F32)],
        compiler_params=_params(("parallel", "parallel", "arbitrary")),
        name="mla_prompt_attention",
    )(qabs, qpe, c, kpe, kst)


def _rows_per_head(x, t):
    nh, n = x.shape
    return jnp.concatenate([jnp.broadcast_to(x[h:h + 1, :], (t, n)) for h in range(nh)], axis=0)


def _sattn_body(pt_ref, qa_ref, qp_ref, cn_ref, kn_ref, snt_ref, ckv_hbm, kpet_hbm, kst_hbm, olat_hbm, o_ref,
                cbuf, kbuf, sbuf, sem, m_s, l_s, acc_s, *, n_pages, group, sub_pages, scale):
    del olat_hbm
    b = pl.program_id(0)
    nb = pl.num_programs(0)
    _, nh, t, kv = qa_ref.shape
    rope = qp_ref.shape[3]
    page = cbuf.shape[2]
    ng = n_pages // group
    m_rows = nh * t

    def copies(bb, g, slot, p):
        phys = pt_ref[bb * n_pages + g * group + p]
        return (pltpu.make_async_copy(ckv_hbm.at[phys], cbuf.at[slot, p], sem.at[slot, 0]),
                pltpu.make_async_copy(kpet_hbm.at[phys], kbuf.at[slot, p], sem.at[slot, 1]),
                pltpu.make_async_copy(kst_hbm.at[phys], sbuf.at[slot, p], sem.at[slot, 2]))

    def start_group(bb, g, slot):
        for p in range(group):
            for cp in copies(bb, g, slot, p):
                cp.start()

    def wait_group(slot):
        for p in range(group):
            for cp in copies(0, 0, slot, p):
                cp.wait()

    last = nb * ng - 1

    @pl.when(b == 0)
    def _():
        start_group(0, 0, 0)

    qa = qa_ref[0].reshape(m_rows, kv).astype(BF16)
    qp = qp_ref[0].reshape(m_rows, rope).astype(BF16)
    m_s[...] = jnp.full_like(m_s, -jnp.inf)
    l_s[...] = jnp.zeros_like(l_s)
    acc_s[...] = jnp.zeros_like(acc_s)

    def grp(g, carry):
        step = b * ng + g
        slot = step % 2
        nxt = jnp.minimum(step + 1, last)
        start_group(nxt // ng, nxt % ng, 1 - slot)
        wait_group(slot)
        state = (m_s[...], l_s[...], acc_s[...])
        for p0 in range(0, group, sub_pages):
            pages = range(p0, p0 + sub_pages)
            cb = cbuf[slot, p0:p0 + sub_pages].reshape(sub_pages * page, kv).astype(BF16)
            s_rope = jnp.concatenate([_dot(qp, kbuf[slot, p].astype(BF16)) for p in pages], axis=1)
            ksx = jnp.concatenate([_rows_per_head(sbuf[slot, p], t) for p in pages], axis=1)
            s = ((_dot_nt(qa, cb) + s_rope) * ksx) * scale
            state = _softmax_step(s, *state, cb)
        m_s[...], l_s[...], acc_s[...] = state
        return carry

    lax.fori_loop(0, ng, grp, 0)

    @pl.when(b == nb - 1)
    def _():
        wait_group(1 - last % 2)

    cnb = cn_ref[...].astype(BF16)
    s = ((_dot_nt(qa, cnb) + _dot_nt(qp, kn_ref[...].astype(BF16))) * _rows_per_head(snt_ref[0], t)) * scale
    q_t = lax.broadcasted_iota(I32, (m_rows, t), 0) % t
    k_t = lax.broadcasted_iota(I32, (m_rows, t), 1)
    _, l, acc = _softmax_step(jnp.where(k_t <= q_t, s, -jnp.inf), m_s[...], l_s[...], acc_s[...], cnb)
    o_ref[0] = (acc / l).reshape(nh, t, kv)


def _sample_attention(olat, qabs, qpe, c, kpe, kst_new, row0, n_seq, page_table, cache_ckv, cache_kpet, cache_kst):
    _, nh, t, kv = qabs.shape
    rope = qpe.shape[3]
    n_pages = page_table.shape[1]
    page = cache_ckv.shape[1]
    group = math.gcd(PAGES_PER_GROUP, n_pages)
    g0 = row0 // t
    scale = math.log2(math.e) / math.sqrt(QK_NOPE + rope)
    any_spec = pl.BlockSpec(memory_space=pl.ANY)
    grid_spec = pltpu.PrefetchScalarGridSpec(
        num_scalar_prefetch=1,
        grid=(n_seq,),
        in_specs=[pl.BlockSpec((1, nh, t, kv), lambda b, pt: (g0 + b, 0, 0, 0)),
                  pl.BlockSpec((1, nh, t, rope), lambda b, pt: (g0 + b, 0, 0, 0)),
                  pl.BlockSpec((t, kv), lambda b, pt: (g0 + b, 0)),
                  pl.BlockSpec((t, rope), lambda b, pt: (g0 + b, 0)),
                  pl.BlockSpec((1, nh, t), lambda b, pt: (b, 0, 0)),
                  any_spec, any_spec, any_spec, any_spec],
        out_specs=pl.BlockSpec((1, nh, t, kv), lambda b, pt: (g0 + b, 0, 0, 0)),
        scratch_shapes=[pltpu.VMEM((2, group, page, kv), F32), pltpu.VMEM((2, group, rope, page), F32),
                        pltpu.VMEM((2, group, nh, page), F32), pltpu.SemaphoreType.DMA((2, 3)),
                        pltpu.VMEM((nh * t, 1), F32), pltpu.VMEM((nh * t, 1), F32), pltpu.VMEM((nh * t, kv), F32)],
    )
    return pl.pallas_call(
        functools.partial(_sattn_body, n_pages=n_pages, group=group, sub_pages=math.gcd(PAGES_PER_SOFTMAX, group),
                          scale=scale),
        grid_spec=grid_spec,
        out_shape=jax.ShapeDtypeStruct(olat.shape, F32),
        input_output_aliases={9: 0},
        compiler_params=_params(("arbitrary",)),
        name="mla_sample_attention",
    )(page_table.reshape(-1), qabs, qpe, c, kpe, kst_new, cache_ckv, cache_kpet, cache_kst, olat)


def _merge_body(x_ref, yrg_ref, ol_ref, wuv_ref, g1_ref, g2_ref, wout_ref, gf_ref, wr_ref, br_ref,
                h_ref, xn_ref, ti_ref, tg_ref):
    tm = x_ref.shape[0]
    _, nh, _, kv = ol_ref.shape
    w1 = yrg_ref.shape[1]
    n_exp = wr_ref.shape[1]
    o = jnp.concatenate([_dot(ol_ref[:, h].reshape(tm, kv).astype(BF16), wuv_ref[h]) for h in range(nh)], axis=1)
    y1 = (_rms(yrg_ref[...]) * g1_ref[...]).astype(BF16)
    y2 = (_rms(o) * g2_ref[...]).astype(BF16)
    hh = x_ref[...] + (_dot(y1, wout_ref[:w1, :]) + _dot(y2, wout_ref[w1:, :]))
    h_ref[...] = hh
    xn = _rms(hh) * gf_ref[...]
    xn_ref[...] = xn
    logits = _dot(xn.astype(BF16), wr_ref[...]) + br_ref[...]
    lane = lax.broadcasted_iota(I32, (tm, n_exp), 1)
    vals, idxs = [], []
    for _ in range(TOP_K):
        m = jnp.max(logits, axis=-1, keepdims=True)
        idx = jnp.min(jnp.where(logits == m, lane, n_exp), axis=-1, keepdims=True)
        vals.append(m)
        idxs.append(idx)
        logits = jnp.where(lane == idx, -jnp.inf, logits)
    es = [jnp.exp(v - vals[0]) for v in vals]
    den = es[0]
    for e in es[1:]:
        den = den + e
    lane_k = lax.broadcasted_iota(I32, (tm, TOP_K), 1)
    ti = jnp.zeros((tm, TOP_K), I32)
    tg = jnp.zeros((tm, TOP_K), F32)
    for k in range(TOP_K):
        ti = jnp.where(lane_k == k, idxs[k], ti)
        tg = jnp.where(lane_k == k, es[k] / den, tg)
    ti_ref[...] = ti
    tg_ref[...] = tg


def _merge_route(x, yrg, olat, wuv_bf, g1, g2, wout_bf, gf, wr_bf, b_router):
    n, d = x.shape
    w1 = yrg.shape[1]
    _, nh, _, kv = olat.shape
    w2 = wuv_bf.shape[0] * wuv_bf.shape[2]
    n_exp = wr_bf.shape[1]
    tm = min(TOKEN_TILE, n)
    row = lambda i: (i, 0)
    return pl.pallas_call(
        _merge_body,
        grid=(n // tm,),
        in_specs=[pl.BlockSpec((tm, d), row), pl.BlockSpec((tm, w1), row),
                  pl.BlockSpec((tm // SUBLANES, nh, SUBLANES, kv), lambda i: (i, 0, 0, 0)),
                  _const_spec(wuv_bf.shape), _const_spec((1, w1)), _const_spec((1, w2)), _const_spec(wout_bf.shape),
                  _const_spec((1, d)), _const_spec(wr_bf.shape), _const_spec((1, n_exp))],
        out_specs=[pl.BlockSpec((tm, d), row), pl.BlockSpec((tm, d), row), pl.BlockSpec((tm, TOP_K), row),
                   pl.BlockSpec((tm, TOP_K), row)],
        out_shape=[jax.ShapeDtypeStruct((n, d), F32), jax.ShapeDtypeStruct((n, d), F32),
                   jax.ShapeDtypeStruct((n, TOP_K), I32), jax.ShapeDtypeStruct((n, TOP_K), F32)],
        compiler_params=_params(("parallel",)),
        name="merge_route",
    )(x, yrg, olat, wuv_bf, g1.reshape(1, w1), g2.reshape(1, w2), wout_bf, gf.reshape(1, d), wr_bf,
      b_router.reshape(1, n_exp))


def _moe_plan(top_i, n_exp):
    n, k = top_i.shape
    n_sub = -(-n * k // MOE_SUB) + n_exp
    n_rows = n_sub * MOE_SUB
    n_wi = n_exp + n_sub // MOE_CAP_SUB
    flat = top_i.reshape(-1)
    onehot = (flat[:, None] == jnp.arange(n_exp, dtype=I32)[None, :]).astype(I32)
    csum = jnp.cumsum(onehot, axis=0)
    counts = csum[-1]
    rank = jnp.sum(onehot * csum, axis=1) - 1
    nsub_e = (counts + MOE_SUB - 1) // MOE_SUB
    sub_off = jnp.cumsum(nsub_e) - nsub_e
    pos = (sub_off * MOE_SUB)[flat] + rank
    row_token = jnp.zeros((n_rows,), I32).at[pos].set(jnp.arange(n * k, dtype=I32) // k, unique_indices=True)
    nwi_e = (nsub_e + MOE_CAP_SUB - 1) // MOE_CAP_SUB
    wi_end = jnp.cumsum(nwi_e)
    w = jnp.arange(n_wi, dtype=I32)
    e_w = jnp.minimum(jnp.sum((w[:, None] >= wi_end[None, :]).astype(I32), axis=1), n_exp - 1)
    local = w - (wi_end - nwi_e)[e_w]
    valid = w < wi_end[-1]
    first = jnp.where(valid, sub_off[e_w] + local * MOE_CAP_SUB, 0)
    cnt = jnp.where(valid, jnp.clip(nsub_e[e_w] - local * MOE_CAP_SUB, 0, MOE_CAP_SUB), 0)
    last_e = e_w[jnp.maximum(wi_end[-1] - 1, 0)]
    e_w = jnp.where(valid, e_w, last_e)
    n_used = jnp.sum(nsub_e).reshape(1)
    return (n_rows, pos.astype(I32), row_token, n_used.astype(I32), e_w.astype(I32), first.astype(I32),
            cnt.astype(I32))


def _dispatch_body(rt_ref, nu_ref, xn_hbm, o_ref, stage, sem):
    tile = pl.program_id(0)
    sub = stage.shape[0]
    base = tile * sub

    def row_copy(r, tok):
        return pltpu.make_async_copy(xn_hbm.at[pl.ds(tok, 1), :], stage.at[pl.ds(r, 1), :], sem.at[0])

    def issue(r, c):
        row_copy(r, rt_ref[base + r]).start()
        return c

    def drain(r, c):
        row_copy(r, 0).wait()
        return c

    @pl.when(tile < nu_ref[0])
    def _():
        lax.fori_loop(0, sub, issue, 0, unroll=DMA_UNROLL)
        lax.fori_loop(0, sub, drain, 0, unroll=DMA_UNROLL)
        o_ref[...] = stage[...].astype(o_ref.dtype)


def _dispatch(xn, row_token, n_used, n_rows):
    n, d = xn.shape
    grid_spec = pltpu.PrefetchScalarGridSpec(
        num_scalar_prefetch=2,
        grid=(n_rows // MOE_SUB,),
        in_specs=[pl.BlockSpec(memory_space=pl.ANY)],
        out_specs=pl.BlockSpec((MOE_SUB, d), lambda i, rt, nu: (jnp.minimum(i, nu[0] - 1), 0)),
        scratch_shapes=[pltpu.VMEM((MOE_SUB, d), xn.dtype), pltpu.SemaphoreType.DMA((1,))],
    )
    return pl.pallas_call(
        _dispatch_body,
        grid_spec=grid_spec,
        out_shape=jax.ShapeDtypeStruct((n_rows, d), BF16),
        compiler_params=_params(("arbitrary",)),
        name="moe_dispatch",
    )(row_token, n_used, xn)


def _expert_body(e_ref, first_ref, cnt_ref, xg_hbm, wg_ref, bg_ref, wu_ref, bu_ref, wd_ref, bd_ref, out_hbm,
                 xbuf, acc, wg_s, wu_s, wd_s, sem):
    w = pl.program_id(0)
    f = pl.program_id(1)
    nw = pl.num_programs(0)
    nf = pl.num_programs(1)
    cnt = cnt_ref[w]
    slot = w % 2
    sub = MOE_SUB
    d = acc.shape[1]

    def rows_in(item, half, s):
        return pltpu.make_async_copy(xg_hbm.at[pl.ds((first_ref[item] + s) * sub, sub), :],
                                     xbuf.at[half, pl.ds(s * sub, sub), :], sem.at[half])

    def rows_out(item, s):
        return pltpu.make_async_copy(acc.at[pl.ds(s * sub, sub), :],
                                     out_hbm.at[pl.ds((first_ref[item] + s) * sub, sub), :], sem.at[2])

    def each(n, fn):
        def body(s, c):
            fn(s)
            return c
        lax.fori_loop(0, n, body, 0)

    @pl.when(f == 0)
    def _():
        @pl.when(w == 0)
        def _():
            each(cnt, lambda s: rows_in(0, 0, s).start())

        each(cnt, lambda s: rows_in(w, slot, s).wait())

        @pl.when(w + 1 < nw)
        def _():
            each(cnt_ref[w + 1], lambda s: rows_in(w + 1, 1 - slot, s).start())

        @pl.when(w > 0)
        def _():
            each(cnt_ref[w - 1], lambda s: rows_out(w - 1, s).wait())

    @pl.when(cnt > 0)
    def _():
        wg_s[...] = wg_ref[0].astype(BF16)
        wu_s[...] = wu_ref[0].astype(BF16)
        wd_s[...] = wd_ref[0].astype(BF16)
        bg = bg_ref[0]
        bu = bu_ref[0]

        @pl.when(f == 0)
        def _():
            bd = jnp.broadcast_to(bd_ref[0], (sub, d))

            def init(s):
                acc[pl.ds(pl.multiple_of(s * sub, sub), sub), :] = bd

            each(cnt, init)

        def block(s0, nsub):
            rows = nsub * sub
            r0 = pl.multiple_of(s0 * sub, sub)
            xs = xbuf[slot, pl.ds(r0, rows), :]
            g = jnp.minimum(_dot(xs, wg_s[...]) + bg, SWIGLU_LIMIT)
            u = jnp.clip(_dot(xs, wu_s[...]) + bu, -SWIGLU_LIMIT, SWIGLU_LIMIT)
            act = ((u + 1.0) * g * jax.nn.sigmoid(SWIGLU_ALPHA * g)).astype(BF16)
            for c0 in range(0, d, MOE_DOWN_COLS):
                acc[pl.ds(r0, rows), c0:c0 + MOE_DOWN_COLS] += _dot(act, wd_s[:, c0:c0 + MOE_DOWN_COLS])

            @pl.when(f == nf - 1)
            def _():
                for i in range(nsub):
                    rows_out(w, s0 + i).start()

        n_big = cnt // MOE_BLOCK_SUBS
        each(n_big, lambda i: block(i * MOE_BLOCK_SUBS, MOE_BLOCK_SUBS))
        s0 = n_big * MOE_BLOCK_SUBS
        nsub = MOE_BLOCK_SUBS // 2
        while nsub >= 1:
            @pl.when((cnt & nsub) != 0)
            def _(s0=s0, nsub=nsub):
                block(s0, nsub)

            s0 = s0 + (cnt & nsub)
            nsub //= 2

    @pl.when((f == nf - 1) & (w == nw - 1))
    def _():
        each(cnt, lambda s: rows_out(w, s).wait())


def _experts(xg, e_w, first, cnt, w_gate, b_gate, w_up, b_up, w_down, b_down):
    n_rows, d = xg.shape
    n_exp, _, de = w_gate.shape
    n_wi = e_w.shape[0]
    fch = min(MOE_F, de)
    cap = MOE_SUB * MOE_CAP_SUB
    nf = de // fch

    def fblk(w, f, c):
        return jnp.where(c[w] > 0, f, nf - 1)

    grid_spec = pltpu.PrefetchScalarGridSpec(
        num_scalar_prefetch=3,
        grid=(n_wi, nf),
        in_specs=[pl.BlockSpec(memory_space=pl.ANY),
                  pl.BlockSpec((1, d, fch), lambda w, f, e, a, c: (e[w], 0, fblk(w, f, c))),
                  pl.BlockSpec((1, 1, fch), lambda w, f, e, a, c: (e[w], 0, fblk(w, f, c))),
                  pl.BlockSpec((1, d, fch), lambda w, f, e, a, c: (e[w], 0, fblk(w, f, c))),
                  pl.BlockSpec((1, 1, fch), lambda w, f, e, a, c: (e[w], 0, fblk(w, f, c))),
                  pl.BlockSpec((1, fch, d), lambda w, f, e, a, c: (e[w], fblk(w, f, c), 0)),
                  pl.BlockSpec((1, 1, d), lambda w, f, e, a, c: (e[w], 0, 0))],
        out_specs=pl.BlockSpec(memory_space=pl.ANY),
        scratch_shapes=[pltpu.VMEM((2, cap, d), BF16), pltpu.VMEM((cap, d), F32), pltpu.VMEM((d, fch), BF16),
                        pltpu.VMEM((d, fch), BF16), pltpu.VMEM((fch, d), BF16), pltpu.SemaphoreType.DMA((3,))],
    )
    return pl.pallas_call(
        _expert_body,
        grid_spec=grid_spec,
        out_shape=jax.ShapeDtypeStruct((n_rows, d), F32),
        compiler_params=_params(("arbitrary", "arbitrary")),
        name="moe_experts",
    )(e_w, first, cnt, xg, w_gate, b_gate.reshape(n_exp, 1, de), w_up, b_up.reshape(n_exp, 1, de), w_down,
      b_down.reshape(n_exp, 1, d))


def _combine_body(pos_ref, h_ref, tg_ref, eo_hbm, y_ref, rows, sem):
    tile = pl.program_id(0)
    tm = h_ref.shape[0]
    base = tile * tm

    def row_copy(n, k, r):
        return pltpu.make_async_copy(eo_hbm.at[pl.ds(r, 1), :], rows.at[k, pl.ds(n, 1), :], sem.at[0])

    def issue(n, c):
        for k in range(TOP_K):
            row_copy(n, k, pos_ref[(base + n) * TOP_K + k]).start()
        return c

    def drain(n, c):
        for k in range(TOP_K):
            row_copy(n, k, 0).wait()
        return c

    lax.fori_loop(0, tm, issue, 0)
    lax.fori_loop(0, tm, drain, 0)
    tg = tg_ref[...]
    out = tg[:, 0:1] * rows[0]
    for k in range(1, TOP_K):
        out = out + tg[:, k:k + 1] * rows[k]
    y_ref[...] = h_ref[...] + out


def _combine(h, tg, pos, eo):
    n, d = h.shape
    tm = min(COMBINE_TILE, n)
    grid_spec = pltpu.PrefetchScalarGridSpec(
        num_scalar_prefetch=1,
        grid=(n // tm,),
        in_specs=[pl.BlockSpec((tm, d), lambda i, p: (i, 0)), pl.BlockSpec((tm, TOP_K), lambda i, p: (i, 0)),
                  pl.BlockSpec(memory_space=pl.ANY)],
        out_specs=pl.BlockSpec((tm, d), lambda i, p: (i, 0)),
        scratch_shapes=[pltpu.VMEM((TOP_K, tm, d), F32), pltpu.SemaphoreType.DMA((1,))],
    )
    return pl.pallas_call(
        _combine_body,
        grid_spec=grid_spec,
        out_shape=jax.ShapeDtypeStruct((n, d), F32),
        compiler_params=_params(("arbitrary",)),
        name="moe_combine",
    )(pos, h, tg, eo)


def _moe(h, xn, ti, tg, w_gate, b_gate, w_up, b_up, w_down, b_down):
    n_exp = w_gate.shape[0]
    n_rows, pos, row_token, n_used, e_w, first, cnt = _moe_plan(ti, n_exp)
    xg = _dispatch(xn, row_token, n_used, n_rows)
    eo = _experts(xg, e_w, first, cnt, w_gate, b_gate, w_up, b_up, w_down, b_down)
    return _combine(h, tg, pos, eo)


def _rope_tables(pos, rope, reps):
    half = rope // 2
    inv_freq = jnp.power(ROPE_THETA, -jnp.arange(half, dtype=F32) / half)
    ang = pos.astype(F32)[:, None] * inv_freq[None, :]
    cos, sin = jnp.cos(ang), jnp.sin(ang)
    return (jnp.tile(jnp.concatenate([cos, cos], axis=1), (1, reps)),
            jnp.tile(jnp.concatenate([-sin, sin], axis=1), (1, reps)))


def kernel(x_prompt, x_sample, cache_ckv, cache_kpe, cache_kscale, state_conv, state_h, page_table,
           norm_attn_gain, w_in, conv_w, conv_b, lru_w_a, lru_b_a, lru_w_i, lru_b_i, lru_lambda,
           q_norm_gain, w_uq, kv_norm_gain, w_uk, w_uv, qk_gain_q, qk_gain_k,
           out_norm_rg, out_norm_mla, w_out, norm_ffn_gain, w_router, b_router,
           w_gate, b_gate, w_up, b_up, w_down, b_down):
    nb, seq, d = x_prompt.shape
    db, t, _ = x_sample.shape
    assert t == SUBLANES, "the sample group is laid out one sequence per sublane group"
    lru_w = state_h.shape[1]
    kv_lora, nh, _ = w_uk.shape
    q_lora = w_uq.shape[0]
    rope = cache_kpe.shape[2]
    qk_head = QK_NOPE + rope
    n_p, n_s = nb * seq, db * t
    past_len = page_table.shape[1] * cache_ckv.shape[1]

    x_all = jnp.concatenate([x_prompt.reshape(n_p, d), x_sample.reshape(n_s, d)], axis=0)
    pos = jnp.concatenate([jnp.tile(jnp.arange(seq), nb), jnp.tile(past_len + jnp.arange(t), db)])
    cos_w, sin_w = _rope_tables(pos, rope, nh)

    xrg, gate, ql, kvl, kr = _in_proj(x_all, norm_attn_gain, w_in.astype(BF16),
                                      (lru_w, lru_w, q_lora, kv_lora, rope))

    w_uq3 = w_uq.reshape(q_lora, nh, qk_head)
    wqn_bf = w_uq3[:, :, :QK_NOPE].reshape(q_lora, nh * QK_NOPE).astype(BF16)
    wqr_bf = w_uq3[:, :, QK_NOPE:].reshape(q_lora, nh * rope).astype(BF16)
    wukt_bf = jnp.transpose(w_uk, (1, 2, 0)).astype(BF16)
    wuk_bf = w_uk.reshape(kv_lora, nh * QK_NOPE).astype(BF16)
    qabs, qpe, c, kpe, ks = _mla_proj(
        ql, kvl, kr, cos_w, sin_w, q_norm_gain, wqn_bf, wqr_bf, qk_gain_q[:QK_NOPE],
        jnp.tile(qk_gain_q[QK_NOPE:], nh), qk_gain_k[:QK_NOPE], qk_gain_k[QK_NOPE:], kv_norm_gain,
        wukt_bf, wuk_bf, nh)

    wai_bf = jnp.concatenate([lru_w_a, lru_w_i], axis=-1).astype(BF16)
    y_p, h_p = _rec_prompt(xrg, gate, nb, seq, conv_w, conv_b, wai_bf, lru_b_a, lru_b_i, lru_lambda)
    xrg_s = xrg[n_p:].reshape(db, t, lru_w)
    xpad = jnp.concatenate([state_conv, xrg_s], axis=1)
    x_shift = [xpad[:, k:k + t].reshape(n_s, lru_w) for k in range(CONV_W)]
    y_s, h_s = _rec_sample(x_shift, gate, n_p, state_h, conv_w, conv_b, wai_bf, lru_b_a, lru_b_i, lru_lambda)
    p_conv = xrg[:n_p].reshape(nb, seq, lru_w)[:, seq - (CONV_W - 1):]
    s_conv = xpad[:, t:]

    kst = jnp.transpose(ks[:n_p])
    olat = _prompt_attention(qabs, qpe, c, kpe, kst, nb, seq)
    kst_new = jnp.swapaxes(ks[n_p:].reshape(db, t, nh), 1, 2)
    olat = _sample_attention(olat, qabs, qpe, c, kpe, kst_new, n_p, db, page_table, cache_ckv,
                             jnp.swapaxes(cache_kpe, 1, 2), jnp.swapaxes(cache_kscale, 1, 2))

    wuv_bf = jnp.transpose(w_uv, (1, 0, 2)).astype(BF16)
    hh, xn, ti, tg = _merge_route(x_all, jnp.concatenate([y_p, y_s], axis=0), olat,
                                  wuv_bf, out_norm_rg, out_norm_mla, w_out.astype(BF16), norm_ffn_gain,
                                  w_router.astype(BF16), b_router)
    y_all = _moe(hh, xn, ti, tg, w_gate, b_gate, w_up, b_up, w_down, b_down)

    return (y_all[:n_p].reshape(nb, seq, d), y_all[n_p:].reshape(db, t, d),
            c[:n_p].reshape(nb, seq, kv_lora), kpe[:n_p].reshape(nb, seq, rope), ks[:n_p].reshape(nb, seq, nh),
            p_conv, h_p.reshape(nb, lru_w),
            c[n_p:].reshape(db, t, kv_lora), kpe[n_p:].reshape(db, t, rope), ks[n_p:].reshape(db, t, nh),
            s_conv, h_s)
```

```python
import functools
import math

import jax
import jax.numpy as jnp
from jax import lax
from jax.experimental import pallas as pl
from jax.experimental.pallas import tpu as pltpu

F32 = jnp.float32
BF16 = jnp.bfloat16
I32 = jnp.int32

EPS = 1e-6
LRU_C = 8.0
LRU_BLOCK_W = 128
CONV_W = 4
ROPE_THETA = 10000.0
QK_NOPE = 128
TOP_K = 4
SWIGLU_LIMIT = 7.0
SWIGLU_ALPHA = 1.702

SUBLANES = 8
LANES = 128
VMEM_LIMIT_BYTES = 56 * 1024 * 1024

TOKEN_TILE = 256
SCAN_TILE = 256
ATTN_TQ = 256
ATTN_TK = 512
PAGES_PER_GROUP = 16
PAGES_PER_SOFTMAX = 16
MOE_SUB = 256
MOE_CAP_SUB = 6
MOE_BLOCK_SUBS = 4
MOE_F = 256
MOE_DOWN_COLS = 512
DMA_UNROLL = 8
COMBINE_TILE = 128


def _params(sem, vmem=VMEM_LIMIT_BYTES):
    return pltpu.CompilerParams(dimension_semantics=sem, vmem_limit_bytes=vmem)


def _dot(a, b):
    return jnp.dot(a, b, preferred_element_type=F32)


def _dot_nt(a, b):
    return lax.dot_general(a, b, (((1,), (1,)), ((), ())), preferred_element_type=F32)


def _rms(x):
    return x * lax.rsqrt(jnp.mean(x * x, axis=-1, keepdims=True) + EPS)


def _const_spec(shape):
    nd = len(shape)
    return pl.BlockSpec(shape, lambda *_: (0,) * nd)


def _inproj_body(x_ref, g_ref, w_ref, *out_refs, offs):
    xn = (_rms(x_ref[...]) * g_ref[...]).astype(BF16)
    for o_ref, (a, b) in zip(out_refs, offs):
        o_ref[...] = _dot(xn, w_ref[:, a:b])


def _in_proj(x, gain, w_bf, widths):
    n, d = x.shape
    tm = min(TOKEN_TILE, n)
    offs, a = [], 0
    for w in widths:
        offs.append((a, a + w))
        a += w
    return pl.pallas_call(
        functools.partial(_inproj_body, offs=tuple(offs)),
        grid=(n // tm,),
        in_specs=[pl.BlockSpec((tm, d), lambda i: (i, 0)), _const_spec((1, d)), _const_spec(w_bf.shape)],
        out_specs=[pl.BlockSpec((tm, w), lambda i: (i, 0)) for w in widths],
        out_shape=[jax.ShapeDtypeStruct((n, w), F32) for w in widths],
        compiler_params=_params(("parallel",)),
        name="in_proj",
    )(x, gain.reshape(1, d), w_bf)


def _mla_proj_body(ql_ref, kvl_ref, kr_ref, cos_ref, sin_ref, qng_ref, wqn_ref, wqr_ref, gqn_ref, gqr_ref,
                   gkn_ref, gkr_ref, kvg_ref, wukt_ref, wuk_ref,
                   qabs_ref, qpe_ref, c_ref, kpe_ref, ks_ref, *, nh):
    tm = ql_ref.shape[0]
    rope = kr_ref.shape[1]
    half = rope // 2
    qk_head = QK_NOPE + rope
    g8 = tm // SUBLANES

    qn = (_rms(ql_ref[...]) * qng_ref[...]).astype(BF16)
    qnope = _dot(qn, wqn_ref[...])
    qrope = _dot(qn, wqr_ref[...])
    r2 = qrope * qrope
    lane = lax.broadcasted_iota(I32, (tm, LANES), 1)
    heads_per_vreg = LANES // rope
    inv_heads = []
    for h in range(nh):
        blk = qnope[:, h * QK_NOPE:(h + 1) * QK_NOPE]
        ssn = jnp.sum(blk * blk, axis=-1, keepdims=True)
        v, sub = divmod(h, heads_per_vreg)
        rblk = r2[:, v * LANES:(v + 1) * LANES]
        in_head = (lane >= sub * rope) & (lane < (sub + 1) * rope)
        ssr = jnp.sum(jnp.where(in_head, rblk, 0.0), axis=-1, keepdims=True)
        inv_h = lax.rsqrt((ssn + ssr) / qk_head + EPS)
        inv_heads.append(inv_h)
        qa_in = ((blk * inv_h) * gqn_ref[...]) * gkn_ref[...]
        qabs = _dot(qa_in.astype(BF16), wukt_ref[h])
        qabs_ref[:, h] = qabs.reshape(g8, SUBLANES, qabs.shape[1])

    cols = []
    for v in range(nh // heads_per_vreg):
        col = inv_heads[v * heads_per_vreg]
        for sub in range(1, heads_per_vreg):
            col = jnp.where(lane >= sub * rope, inv_heads[v * heads_per_vreg + sub], col)
        cols.append(col)
    inv_all = jnp.concatenate(cols, axis=1)
    qr = (qrope * inv_all) * gqr_ref[...]
    wide = lax.broadcasted_iota(I32, qr.shape, 1)
    first_half = (wide % rope) < half
    width = qr.shape[1]
    rot = jnp.where(first_half, pltpu.roll(qr, width - half, 1), pltpu.roll(qr, half, 1))
    qpe = qr * cos_ref[...] + rot * sin_ref[...]
    for h in range(nh):
        qpe_ref[:, h] = qpe[:, h * rope:(h + 1) * rope].reshape(g8, SUBLANES, rope)

    c = _rms(kvl_ref[...]) * kvg_ref[...]
    c_ref[...] = c
    kn = _dot(c.astype(BF16), wuk_ref[...])
    kr = kr_ref[...]
    sskr = jnp.sum(kr * kr, axis=-1, keepdims=True)
    lane_h = lax.broadcasted_iota(I32, (tm, nh), 1)
    ks = jnp.zeros((tm, nh), F32)
    for h in range(nh):
        blk = kn[:, h * QK_NOPE:(h + 1) * QK_NOPE]
        ssk = jnp.sum(blk * blk, axis=-1, keepdims=True)
        ks = jnp.where(lane_h == h, lax.rsqrt((ssk + sskr) / qk_head + EPS), ks)
    ks_ref[...] = ks
    krg = kr * gkr_ref[...]
    swapped = jnp.concatenate([krg[:, half:], krg[:, :half]], axis=1)
    kpe_ref[...] = krg * cos_ref[:, :rope] + swapped * sin_ref[:, :rope]


def _mla_proj(ql, kvl, kr, cos_w, sin_w, q_norm_gain, wqn_bf, wqr_bf, gqn, gqr_w, gkn, gkr, kv_norm_gain,
              wukt_bf, wuk_bf, nh):
    n, q_lora = ql.shape
    kv_lora = kvl.shape[1]
    rope = kr.shape[1]
    tm = min(TOKEN_TILE, n)
    g8 = tm // SUBLANES
    row = lambda i: (i, 0)
    row4 = lambda i: (i, 0, 0, 0)
    return pl.pallas_call(
        functools.partial(_mla_proj_body, nh=nh),
        grid=(n // tm,),
        in_specs=[pl.BlockSpec((tm, q_lora), row), pl.BlockSpec((tm, kv_lora), row), pl.BlockSpec((tm, rope), row),
                  pl.BlockSpec((tm, nh * rope), row), pl.BlockSpec((tm, nh * rope), row),
                  _const_spec((1, q_lora)), _const_spec(wqn_bf.shape), _const_spec(wqr_bf.shape),
                  _const_spec((1, QK_NOPE)), _const_spec((1, nh * rope)), _const_spec((1, QK_NOPE)),
                  _const_spec((1, rope)), _const_spec((1, kv_lora)), _const_spec(wukt_bf.shape),
                  _const_spec(wuk_bf.shape)],
        out_specs=[pl.BlockSpec((g8, nh, SUBLANES, kv_lora), row4), pl.BlockSpec((g8, nh, SUBLANES, rope), row4),
                   pl.BlockSpec((tm, kv_lora), row), pl.BlockSpec((tm, rope), row), pl.BlockSpec((tm, nh), row)],
        out_shape=[jax.ShapeDtypeStruct((n // SUBLANES, nh, SUBLANES, kv_lora), F32),
                   jax.ShapeDtypeStruct((n // SUBLANES, nh, SUBLANES, rope), F32),
                   jax.ShapeDtypeStruct((n, kv_lora), F32), jax.ShapeDtypeStruct((n, rope), F32),
                   jax.ShapeDtypeStruct((n, nh), F32)],
        compiler_params=_params(("parallel",)),
        name="mla_proj",
    )(ql, kvl, kr, cos_w, sin_w, q_norm_gain.reshape(1, -1), wqn_bf, wqr_bf, gqn.reshape(1, -1),
      gqr_w.reshape(1, -1), gkn.reshape(1, -1), gkr.reshape(1, -1), kv_norm_gain.reshape(1, -1), wukt_bf, wuk_bf)


def _expm1(x):
    u = jnp.exp(x)
    um1 = u - 1.0
    return jnp.where(u == 1.0, x, jnp.where(um1 == -1.0, -1.0, um1 * x / jnp.log(u)))


def _lru_gates(xc, wai_ref, ba_ref, bi_ref, lam_ref):
    nblk = wai_ref.shape[0]
    xcb = xc.astype(BF16)
    ra, ri = [], []
    for n in range(nblk):
        g = _dot(xcb[:, n * LRU_BLOCK_W:(n + 1) * LRU_BLOCK_W], wai_ref[n])
        ra.append(g[:, :LRU_BLOCK_W])
        ri.append(g[:, LRU_BLOCK_W:])
    r = jax.nn.sigmoid(jnp.concatenate(ra, axis=1) + ba_ref[...])
    i = jax.nn.sigmoid(jnp.concatenate(ri, axis=1) + bi_ref[...])
    log_a = (-LRU_C * r) * jax.nn.softplus(-lam_ref[...])
    a = jnp.exp(log_a)
    mult = jnp.sqrt(jnp.maximum(-_expm1(2.0 * log_a), 0.0))
    return a, mult * i * xc


def _group_scan(a, u, rows):
    for s in (1, 2, 4):
        keep = rows >= s
        a_sh = jnp.where(keep, pltpu.roll(a, s, 0), 1.0)
        u_sh = jnp.where(keep, pltpu.roll(u, s, 0), 0.0)
        u = a * u_sh + u
        a = a * a_sh
    return a, u


def _rec_prompt_body(x_ref, gate_ref, cw_ref, cb_ref, wai_ref, ba_ref, bi_ref, lam_ref, y_ref, ht_ref,
                     xbuf, a_s, u_s, h_s, *, tt):
    j = pl.program_id(1)
    w = x_ref.shape[1]

    @pl.when(j == 0)
    def _():
        xbuf[0:SUBLANES, :] = jnp.zeros((SUBLANES, w), F32)
        h_s[...] = jnp.zeros_like(h_s)

    xbuf[SUBLANES:SUBLANES + tt, :] = x_ref[...]
    xc = cb_ref[...]
    for k in range(CONV_W):
        lo = SUBLANES - (CONV_W - 1) + k
        xc = xc + cw_ref[k:k + 1, :] * xbuf[lo:lo + tt, :]
    xbuf[0:SUBLANES, :] = xbuf[tt:tt + SUBLANES, :]
    a, u = _lru_gates(xc, wai_ref, ba_ref, bi_ref, lam_ref)
    a_s[...] = a
    u_s[...] = u
    rows = lax.broadcasted_iota(I32, (SUBLANES, w), 0)

    def grp(g, h):
        r0 = pl.multiple_of(g * SUBLANES, SUBLANES)
        ag, ug = _group_scan(a_s[pl.ds(r0, SUBLANES), :], u_s[pl.ds(r0, SUBLANES), :], rows)
        hs = ug + ag * h
        y_ref[pl.ds(r0, SUBLANES), :] = jax.nn.gelu(gate_ref[pl.ds(r0, SUBLANES), :]) * hs
        return hs[SUBLANES - 1:SUBLANES, :]

    h = lax.fori_loop(0, tt // SUBLANES, grp, h_s[...])
    h_s[...] = h
    ht_ref[0] = h


def _rec_prompt(xrg, gate, n_seq, seq_len, conv_w, conv_b, wai_bf, b_a, b_i, lam):
    w = xrg.shape[1]
    tt = min(SCAN_TILE, seq_len)
    nt = seq_len // tt
    row = lambda b, j: (b * nt + j, 0)
    vec = lambda a: a.reshape(1, w)
    return pl.pallas_call(
        functools.partial(_rec_prompt_body, tt=tt),
        grid=(n_seq, nt),
        in_specs=[pl.BlockSpec((tt, w), row), pl.BlockSpec((tt, w), row), _const_spec((CONV_W, w)),
                  _const_spec((1, w)), _const_spec(wai_bf.shape), _const_spec((1, w)), _const_spec((1, w)),
                  _const_spec((1, w))],
        out_specs=[pl.BlockSpec((tt, w), row), pl.BlockSpec((1, 1, w), lambda b, j: (b, 0, 0))],
        out_shape=[jax.ShapeDtypeStruct((n_seq * seq_len, w), F32), jax.ShapeDtypeStruct((n_seq, 1, w), F32)],
        scratch_shapes=[pltpu.VMEM((tt + SUBLANES, w), F32), pltpu.VMEM((tt, w), F32), pltpu.VMEM((tt, w), F32),
                        pltpu.VMEM((1, w), F32)],
        compiler_params=_params(("parallel", "arbitrary")),
        name="rglru_prompt",
    )(xrg, gate, conv_w, vec(conv_b), wai_bf, vec(b_a), vec(b_i), vec(lam))


def _rec_sample_body(x0_ref, x1_ref, x2_ref, x3_ref, gate_ref, h0_ref, cw_ref, cb_ref, wai_ref, ba_ref, bi_ref,
                     lam_ref, y_ref, ht_ref, a_s, u_s):
    tm, w = gate_ref.shape
    xc = cb_ref[...]
    for k, x_ref in enumerate((x0_ref, x1_ref, x2_ref, x3_ref)):
        xc = xc + cw_ref[k:k + 1, :] * x_ref[...]
    a, u = _lru_gates(xc, wai_ref, ba_ref, bi_ref, lam_ref)
    a_s[...] = a
    u_s[...] = u
    rows = lax.broadcasted_iota(I32, (SUBLANES, w), 0)

    def grp(g, carry):
        r0 = pl.multiple_of(g * SUBLANES, SUBLANES)
        ag, ug = _group_scan(a_s[pl.ds(r0, SUBLANES), :], u_s[pl.ds(r0, SUBLANES), :], rows)
        hs = ug + ag * h0_ref[pl.ds(g, 1), :]
        y_ref[pl.ds(r0, SUBLANES), :] = jax.nn.gelu(gate_ref[pl.ds(r0, SUBLANES), :]) * hs
        ht_ref[pl.ds(g, 1), :] = hs[SUBLANES - 1:SUBLANES, :]
        return carry

    lax.fori_loop(0, tm // SUBLANES, grp, 0)


def _rec_sample(x_shift, gate, gate_row0, h0, conv_w, conv_b, wai_bf, b_a, b_i, lam):
    n, w = x_shift[0].shape
    tm = min(TOKEN_TILE, n)
    g8 = tm // SUBLANES
    g0 = gate_row0 // tm
    row = lambda i: (i, 0)
    vec = lambda a: a.reshape(1, w)
    return pl.pallas_call(
        _rec_sample_body,
        grid=(n // tm,),
        in_specs=[pl.BlockSpec((tm, w), row)] * CONV_W
        + [pl.BlockSpec((tm, w), lambda i: (g0 + i, 0)), pl.BlockSpec((g8, w), row), _const_spec((CONV_W, w)),
           _const_spec((1, w)), _const_spec(wai_bf.shape), _const_spec((1, w)), _const_spec((1, w)),
           _const_spec((1, w))],
        out_specs=[pl.BlockSpec((tm, w), row), pl.BlockSpec((g8, w), row)],
        out_shape=[jax.ShapeDtypeStruct((n, w), F32), jax.ShapeDtypeStruct((n // SUBLANES, w), F32)],
        scratch_shapes=[pltpu.VMEM((tm, w), F32), pltpu.VMEM((tm, w), F32)],
        compiler_params=_params(("parallel",)),
        name="rglru_sample",
    )(*x_shift, gate, h0, conv_w, vec(conv_b), wai_bf, vec(b_a), vec(b_i), vec(lam))


def _fold_lanes(x, op):
    n = x.shape[1]
    if n % LANES != 0 or n == LANES:
        return x
    out = x[:, :LANES]
    for c0 in range(LANES, n, LANES):
        out = op(out, x[:, c0:c0 + LANES])
    return out


def _softmax_step(s, m, l, acc, cb):
    m_new = jnp.maximum(m, jnp.max(_fold_lanes(s, jnp.maximum), axis=-1, keepdims=True))
    corr = jnp.exp2(m - m_new)
    p = jnp.exp2(s - m_new)
    l = l * corr + jnp.sum(_fold_lanes(p, jnp.add), axis=-1, keepdims=True)
    acc = acc * corr + _dot(p.astype(BF16), cb)
    return m_new, l, acc


def _pattn_body(qa_ref, qp_ref, c_ref, kp_ref, kst_ref, o_ref, qa_s, qp_s, m_s, l_s, acc_s, *, tq, tk, scale):
    i = pl.program_id(1)
    j = pl.program_id(2)
    nh = qa_s.shape[0]
    kv = qa_s.shape[2]
    rope = qp_s.shape[2]

    @pl.when(j == 0)
    def _():
        for h in range(nh):
            qa_s[h] = qa_ref[:, h].reshape(tq, kv).astype(BF16)
            qp_s[h] = qp_ref[:, h].reshape(tq, rope).astype(BF16)
        m_s[...] = jnp.full_like(m_s, -jnp.inf)
        l_s[...] = jnp.zeros_like(l_s)
        acc_s[...] = jnp.zeros_like(acc_s)

    def kv_tile(masked):
        cb = c_ref[...].astype(BF16)
        kb = kp_ref[...].astype(BF16)
        key_scale = kst_ref[...] * scale
        if masked:
            qpos = i * tq + lax.broadcasted_iota(I32, (tq, tk), 0)
            kpos = j * tk + lax.broadcasted_iota(I32, (tq, tk), 1)
            visible = kpos <= qpos

        def logits(h):
            return _dot_nt(qa_s[h], cb) + _dot_nt(qp_s[h], kb)

        qk_next = logits(0)
        for h in range(nh):
            s = qk_next * key_scale[h:h + 1, :]
            if h + 1 < nh:
                qk_next = logits(h + 1)
            if masked:
                s = jnp.where(visible, s, -jnp.inf)
            m_s[h], l_s[h], acc_s[h] = _softmax_step(s, m_s[h], l_s[h], acc_s[h], cb)

    active = j * tk < (i + 1) * tq
    crosses_diagonal = (j + 1) * tk - 1 > i * tq

    @pl.when(active & crosses_diagonal)
    def _():
        kv_tile(True)

    @pl.when(active & jnp.logical_not(crosses_diagonal))
    def _():
        kv_tile(False)

    @pl.when(j == pl.num_programs(2) - 1)
    def _():
        for h in range(nh):
            o_ref[:, h] = (acc_s[h] / l_s[h]).reshape(tq // SUBLANES, SUBLANES, kv)


def _prompt_attention(qabs, qpe, c, kpe, kst, n_seq, seq_len):
    n_groups, nh, _, kv = qabs.shape
    rope = qpe.shape[3]
    tq = min(ATTN_TQ, seq_len)
    tk = min(ATTN_TK, seq_len)
    nq, nk = seq_len // tq, seq_len // tk
    g8 = tq // SUBLANES
    scale = math.log2(math.e) / math.sqrt(QK_NOPE + rope)

    def kblk(i, j):
        return jnp.minimum(j, ((i + 1) * tq - 1) // tk)

    return pl.pallas_call(
        functools.partial(_pattn_body, tq=tq, tk=tk, scale=scale),
        grid=(n_seq, nq, nk),
        in_specs=[pl.BlockSpec((g8, nh, SUBLANES, kv), lambda b, i, j: (b * nq + i, 0, 0, 0)),
                  pl.BlockSpec((g8, nh, SUBLANES, rope), lambda b, i, j: (b * nq + i, 0, 0, 0)),
                  pl.BlockSpec((tk, kv), lambda b, i, j: (b * nk + kblk(i, j), 0)),
                  pl.BlockSpec((tk, rope), lambda b, i, j: (b * nk + kblk(i, j), 0)),
                  pl.BlockSpec((nh, tk), lambda b, i, j: (0, b * nk + kblk(i, j)))],
        out_specs=pl.BlockSpec((g8, nh, SUBLANES, kv), lambda b, i, j: (b * nq + i, 0, 0, 0)),
        out_shape=jax.ShapeDtypeStruct((n_groups, nh, SUBLANES, kv), F32),
        scratch_shapes=[pltpu.VMEM((nh, tq, kv), BF16), pltpu.VMEM((nh, tq, rope), BF16),
                        pltpu.VMEM((nh, tq, 1), F32), pltpu.VMEM((nh, tq, 1), F32), pltpu.VMEM((nh, tq, kv), F32)],
        compiler_params=_params(("parallel", "parallel", "arbitrary")),
        name="mla_prompt_attention",
    )(qabs, qpe, c, kpe, kst)


def _rows_per_head(x, t):
    nh, n = x.shape
    return jnp.concatenate([jnp.broadcast_to(x[h:h + 1, :], (t, n)) for h in range(nh)], axis=0)


def _sattn_body(pt_ref, qa_ref, qp_ref, cn_ref, kn_ref, snt_ref, ckv_hbm, kpet_hbm, kst_hbm, olat_hbm, o_ref,
                cbuf, kbuf, sbuf, sem, m_s, l_s, acc_s, *, n_pages, group, sub_pages, scale):
    del olat_hbm
    b = pl.program_id(0)
    nb = pl.num_programs(0)
    _, nh, t, kv = qa_ref.shape
    rope = qp_ref.shape[3]
    page = cbuf.shape[2]
    ng = n_pages // group
    m_rows = nh * t

    def copies(bb, g, slot, p):
        phys = pt_ref[bb * n_pages + g * group + p]
        return (pltpu.make_async_copy(ckv_hbm.at[phys], cbuf.at[slot, p], sem.at[slot, 0]),
                pltpu.make_async_copy(kpet_hbm.at[phys], kbuf.at[slot, p], sem.at[slot, 1]),
                pltpu.make_async_copy(kst_hbm.at[phys], sbuf.at[slot, p], sem.at[slot, 2]))

    def start_group(bb, g, slot):
        for p in range(group):
            for cp in copies(bb, g, slot, p):
                cp.start(priority=p % 2)

    def wait_group(slot):
        for p in range(group):
            for cp in copies(0, 0, slot, p):
                cp.wait()

    last = nb * ng - 1

    @pl.when(b == 0)
    def _():
        start_group(0, 0, 0)

    qa = qa_ref[0].reshape(m_rows, kv).astype(BF16)
    qp = qp_ref[0].reshape(m_rows, rope).astype(BF16)
    m_s[...] = jnp.full_like(m_s, -jnp.inf)
    l_s[...] = jnp.zeros_like(l_s)
    acc_s[...] = jnp.zeros_like(acc_s)

    def grp(g, carry):
        step = b * ng + g
        slot = step % 2
        nxt = jnp.minimum(step + 1, last)
        start_group(nxt // ng, nxt % ng, 1 - slot)
        wait_group(slot)
        state = (m_s[...], l_s[...], acc_s[...])
        for p0 in range(0, group, sub_pages):
            pages = range(p0, p0 + sub_pages)
            cb = cbuf[slot, p0:p0 + sub_pages].reshape(sub_pages * page, kv).astype(BF16)
            s_rope = jnp.concatenate([_dot(qp, kbuf[slot, p].astype(BF16)) for p in pages], axis=1)
            ksx = jnp.concatenate([_rows_per_head(sbuf[slot, p], t) for p in pages], axis=1)
            s = ((_dot_nt(qa, cb) + s_rope) * ksx) * scale
            state = _softmax_step(s, *state, cb)
        m_s[...], l_s[...], acc_s[...] = state
        return carry

    lax.fori_loop(0, ng, grp, 0)

    @pl.when(b == nb - 1)
    def _():
        wait_group(1 - last % 2)

    cnb = cn_ref[...].astype(BF16)
    s = ((_dot_nt(qa, cnb) + _dot_nt(qp, kn_ref[...].astype(BF16))) * _rows_per_head(snt_ref[0], t)) * scale
    q_t = lax.broadcasted_iota(I32, (m_rows, t), 0) % t
    k_t = lax.broadcasted_iota(I32, (m_rows, t), 1)
    _, l, acc = _softmax_step(jnp.where(k_t <= q_t, s, -jnp.inf), m_s[...], l_s[...], acc_s[...], cnb)
    o_ref[0] = (acc / l).reshape(nh, t, kv)


def _sample_attention(olat, qabs, qpe, c, kpe, kst_new, row0, n_seq, page_table, cache_ckv, cache_kpet, cache_kst):
    _, nh, t, kv = qabs.shape
    rope = qpe.shape[3]
    n_pages = page_table.shape[1]
    page = cache_ckv.shape[1]
    group = math.gcd(PAGES_PER_GROUP, n_pages)
    g0 = row0 // t
    scale = math.log2(math.e) / math.sqrt(QK_NOPE + rope)
    any_spec = pl.BlockSpec(memory_space=pl.ANY)
    grid_spec = pltpu.PrefetchScalarGridSpec(
        num_scalar_prefetch=1,
        grid=(n_seq,),
        in_specs=[pl.BlockSpec((1, nh, t, kv), lambda b, pt: (g0 + b, 0, 0, 0)),
                  pl.BlockSpec((1, nh, t, rope), lambda b, pt: (g0 + b, 0, 0, 0)),
                  pl.BlockSpec((t, kv), lambda b, pt: (g0 + b, 0)),
                  pl.BlockSpec((t, rope), lambda b, pt: (g0 + b, 0)),
                  pl.BlockSpec((1, nh, t), lambda b, pt: (b, 0, 0)),
                  any_spec, any_spec, any_spec, any_spec],
        out_specs=pl.BlockSpec((1, nh, t, kv), lambda b, pt: (g0 + b, 0, 0, 0)),
        scratch_shapes=[pltpu.VMEM((2, group, page, kv), F32), pltpu.VMEM((2, group, rope, page), F32),
                        pltpu.VMEM((2, group, nh, page), F32), pltpu.SemaphoreType.DMA((2, 3)),
                        pltpu.VMEM((nh * t, 1), F32), pltpu.VMEM((nh * t, 1), F32), pltpu.VMEM((nh * t, kv), F32)],
    )
    return pl.pallas_call(
        functools.partial(_sattn_body, n_pages=n_pages, group=group, sub_pages=math.gcd(PAGES_PER_SOFTMAX, group),
                          scale=scale),
        grid_spec=grid_spec,
        out_shape=jax.ShapeDtypeStruct(olat.shape, F32),
        input_output_aliases={9: 0},
        compiler_params=_params(("arbitrary",)),
        name="mla_sample_attention",
    )(page_table.reshape(-1), qabs, qpe, c, kpe, kst_new, cache_ckv, cache_kpet, cache_kst, olat)


def _merge_body(x_ref, yrg_ref, ol_ref, wuv_ref, g1_ref, g2_ref, wout_ref, gf_ref, wr_ref, br_ref,
                h_ref, xn_ref, ti_ref, tg_ref):
    tm = x_ref.shape[0]
    _, nh, _, kv = ol_ref.shape
    w1 = yrg_ref.shape[1]
    n_exp = wr_ref.shape[1]
    o = jnp.concatenate([_dot(ol_ref[:, h].reshape(tm, kv).astype(BF16), wuv_ref[h]) for h in range(nh)], axis=1)
    y1 = (_rms(yrg_ref[...]) * g1_ref[...]).astype(BF16)
    y2 = (_rms(o) * g2_ref[...]).astype(BF16)
    hh = x_ref[...] + (_dot(y1, wout_ref[:w1, :]) + _dot(y2, wout_ref[w1:, :]))
    h_ref[...] = hh
    xn = _rms(hh) * gf_ref[...]
    xn_ref[...] = xn
    logits = _dot(xn.astype(BF16), wr_ref[...]) + br_ref[...]
    lane = lax.broadcasted_iota(I32, (tm, n_exp), 1)
    vals, idxs = [], []
    for _ in range(TOP_K):
        m = jnp.max(logits, axis=-1, keepdims=True)
        idx = jnp.min(jnp.where(logits == m, lane, n_exp), axis=-1, keepdims=True)
        vals.append(m)
        idxs.append(idx)
        logits = jnp.where(lane == idx, -jnp.inf, logits)
    es = [jnp.exp(v - vals[0]) for v in vals]
    den = es[0]
    for e in es[1:]:
        den = den + e
    lane_k = lax.broadcasted_iota(I32, (tm, TOP_K), 1)
    ti = jnp.zeros((tm, TOP_K), I32)
    tg = jnp.zeros((tm, TOP_K), F32)
    for k in range(TOP_K):
        ti = jnp.where(lane_k == k, idxs[k], ti)
        tg = jnp.where(lane_k == k, es[k] / den, tg)
    ti_ref[...] = ti
    tg_ref[...] = tg


def _merge_route(x, yrg, olat, wuv_bf, g1, g2, wout_bf, gf, wr_bf, b_router):
    n, d = x.shape
    w1 = yrg.shape[1]
    _, nh, _, kv = olat.shape
    w2 = wuv_bf.shape[0] * wuv_bf.shape[2]
    n_exp = wr_bf.shape[1]
    tm = min(TOKEN_TILE, n)
    row = lambda i: (i, 0)
    return pl.pallas_call(
        _merge_body,
        grid=(n // tm,),
        in_specs=[pl.BlockSpec((tm, d), row), pl.BlockSpec((tm, w1), row),
                  pl.BlockSpec((tm // SUBLANES, nh, SUBLANES, kv), lambda i: (i, 0, 0, 0)),
                  _const_spec(wuv_bf.shape), _const_spec((1, w1)), _const_spec((1, w2)), _const_spec(wout_bf.shape),
                  _const_spec((1, d)), _const_spec(wr_bf.shape), _const_spec((1, n_exp))],
        out_specs=[pl.BlockSpec((tm, d), row), pl.BlockSpec((tm, d), row), pl.BlockSpec((tm, TOP_K), row),
                   pl.BlockSpec((tm, TOP_K), row)],
        out_shape=[jax.ShapeDtypeStruct((n, d), F32), jax.ShapeDtypeStruct((n, d), F32),
                   jax.ShapeDtypeStruct((n, TOP_K), I32), jax.ShapeDtypeStruct((n, TOP_K), F32)],
        compiler_params=_params(("parallel",)),
        name="merge_route",
    )(x, yrg, olat, wuv_bf, g1.reshape(1, w1), g2.reshape(1, w2), wout_bf, gf.reshape(1, d), wr_bf,
      b_router.reshape(1, n_exp))


def _moe_plan(top_i, n_exp):
    n, k = top_i.shape
    n_sub = -(-n * k // MOE_SUB) + n_exp
    n_rows = n_sub * MOE_SUB
    n_wi = n_exp + n_sub // MOE_CAP_SUB
    flat = top_i.reshape(-1)
    onehot = (flat[:, None] == jnp.arange(n_exp, dtype=I32)[None, :]).astype(I32)
    csum = jnp.cumsum(onehot, axis=0)
    counts = csum[-1]
    rank = jnp.sum(onehot * csum, axis=1) - 1
    nsub_e = (counts + MOE_SUB - 1) // MOE_SUB
    sub_off = jnp.cumsum(nsub_e) - nsub_e
    pos = (sub_off * MOE_SUB)[flat] + rank
    row_token = jnp.zeros((n_rows,), I32).at[pos].set(jnp.arange(n * k, dtype=I32) // k, unique_indices=True)
    nwi_e = (nsub_e + MOE_CAP_SUB - 1) // MOE_CAP_SUB
    wi_end = jnp.cumsum(nwi_e)
    w = jnp.arange(n_wi, dtype=I32)
    e_w = jnp.minimum(jnp.sum((w[:, None] >= wi_end[None, :]).astype(I32), axis=1), n_exp - 1)
    local = w - (wi_end - nwi_e)[e_w]
    valid = w < wi_end[-1]
    first = jnp.where(valid, sub_off[e_w] + local * MOE_CAP_SUB, 0)
    cnt = jnp.where(valid, jnp.clip(nsub_e[e_w] - local * MOE_CAP_SUB, 0, MOE_CAP_SUB), 0)
    last_e = e_w[jnp.maximum(wi_end[-1] - 1, 0)]
    e_w = jnp.where(valid, e_w, last_e)
    n_used = jnp.sum(nsub_e).reshape(1)
    return (n_rows, pos.astype(I32), row_token, n_used.astype(I32), e_w.astype(I32), first.astype(I32),
            cnt.astype(I32))


def _dispatch_body(rt_ref, nu_ref, xn_hbm, o_ref, stage, sem):
    tile = pl.program_id(0)
    sub = stage.shape[0]
    base = tile * sub

    def row_copy(r, tok):
        return pltpu.make_async_copy(xn_hbm.at[pl.ds(tok, 1), :], stage.at[pl.ds(r, 1), :], sem.at[0])

    def issue(i, c):
        for u in range(DMA_UNROLL):
            r = i * DMA_UNROLL + u
            row_copy(r, rt_ref[base + r]).start(priority=u % 2)
        return c

    def drain(i, c):
        for u in range(DMA_UNROLL):
            row_copy(i * DMA_UNROLL + u, 0).wait()
        return c

    @pl.when(tile < nu_ref[0])
    def _():
        lax.fori_loop(0, sub // DMA_UNROLL, issue, 0)
        lax.fori_loop(0, sub // DMA_UNROLL, drain, 0)
        o_ref[...] = stage[...].astype(o_ref.dtype)


def _dispatch(xn, row_token, n_used, n_rows):
    n, d = xn.shape
    grid_spec = pltpu.PrefetchScalarGridSpec(
        num_scalar_prefetch=2,
        grid=(n_rows // MOE_SUB,),
        in_specs=[pl.BlockSpec(memory_space=pl.ANY)],
        out_specs=pl.BlockSpec((MOE_SUB, d), lambda i, rt, nu: (jnp.minimum(i, nu[0] - 1), 0)),
        scratch_shapes=[pltpu.VMEM((MOE_SUB, d), xn.dtype), pltpu.SemaphoreType.DMA((1,))],
    )
    return pl.pallas_call(
        _dispatch_body,
        grid_spec=grid_spec,
        out_shape=jax.ShapeDtypeStruct((n_rows, d), BF16),
        compiler_params=_params(("arbitrary",)),
        name="moe_dispatch",
    )(row_token, n_used, xn)


def _expert_body(e_ref, first_ref, cnt_ref, xg_hbm, wg_ref, bg_ref, wu_ref, bu_ref, wd_ref, bd_ref, out_hbm,
                 xbuf, acc, wg_s, wu_s, wd_s, sem):
    w = pl.program_id(0)
    f = pl.program_id(1)
    nw = pl.num_programs(0)
    nf = pl.num_programs(1)
    cnt = cnt_ref[w]
    slot = w % 2
    sub = MOE_SUB
    d = acc.shape[1]

    def rows_in(item, half, s):
        return pltpu.make_async_copy(xg_hbm.at[pl.ds((first_ref[item] + s) * sub, sub), :],
                                     xbuf.at[half, pl.ds(s * sub, sub), :], sem.at[half])

    def rows_out(item, s):
        return pltpu.make_async_copy(acc.at[pl.ds(s * sub, sub), :],
                                     out_hbm.at[pl.ds((first_ref[item] + s) * sub, sub), :], sem.at[2])

    def each(n, fn):
        def body(s, c):
            fn(s)
            return c
        lax.fori_loop(0, n, body, 0)

    @pl.when(f == 0)
    def _():
        @pl.when(w == 0)
        def _():
            each(cnt, lambda s: rows_in(0, 0, s).start())

        each(cnt, lambda s: rows_in(w, slot, s).wait())

        @pl.when(w + 1 < nw)
        def _():
            each(cnt_ref[w + 1], lambda s: rows_in(w + 1, 1 - slot, s).start())

        @pl.when(w > 0)
        def _():
            each(cnt_ref[w - 1], lambda s: rows_out(w - 1, s).wait())

    @pl.when(cnt > 0)
    def _():
        wg_s[...] = wg_ref[0].astype(BF16)
        wu_s[...] = wu_ref[0].astype(BF16)
        wd_s[...] = wd_ref[0].astype(BF16)
        bg = bg_ref[0]
        bu = bu_ref[0]

        @pl.when(f == 0)
        def _():
            bd = jnp.broadcast_to(bd_ref[0], (sub, d))

            def init(s):
                acc[pl.ds(pl.multiple_of(s * sub, sub), sub), :] = bd

            each(cnt, init)

        def block(s0, nsub):
            rows = nsub * sub
            r0 = pl.multiple_of(s0 * sub, sub)
            xs = xbuf[slot, pl.ds(r0, rows), :]
            g = jnp.minimum(_dot(xs, wg_s[...]) + bg, SWIGLU_LIMIT)
            u = jnp.clip(_dot(xs, wu_s[...]) + bu, -SWIGLU_LIMIT, SWIGLU_LIMIT)
            act = ((u + 1.0) * g * jax.nn.sigmoid(SWIGLU_ALPHA * g)).astype(BF16)
            for c0 in range(0, d, MOE_DOWN_COLS):
                acc[pl.ds(r0, rows), c0:c0 + MOE_DOWN_COLS] += _dot(act, wd_s[:, c0:c0 + MOE_DOWN_COLS])

            @pl.when(f == nf - 1)
            def _():
                for i in range(nsub):
                    rows_out(w, s0 + i).start()

        n_big = cnt // MOE_BLOCK_SUBS
        each(n_big, lambda i: block(i * MOE_BLOCK_SUBS, MOE_BLOCK_SUBS))
        s0 = n_big * MOE_BLOCK_SUBS
        nsub = MOE_BLOCK_SUBS // 2
        while nsub >= 1:
            @pl.when((cnt & nsub) != 0)
            def _(s0=s0, nsub=nsub):
                block(s0, nsub)

            s0 = s0 + (cnt & nsub)
            nsub //= 2

    @pl.when((f == nf - 1) & (w == nw - 1))
    def _():
        each(cnt, lambda s: rows_out(w, s).wait())


def _experts(xg, e_w, first, cnt, w_gate, b_gate, w_up, b_up, w_down, b_down):
    n_rows, d = xg.shape
    n_exp, _, de = w_gate.shape
    n_wi = e_w.shape[0]
    fch = min(MOE_F, de)
    cap = MOE_SUB * MOE_CAP_SUB
    nf = de // fch

    def fblk(w, f, c):
        return jnp.where(c[w] > 0, f, nf - 1)

    grid_spec = pltpu.PrefetchScalarGridSpec(
        num_scalar_prefetch=3,
        grid=(n_wi, nf),
        in_specs=[pl.BlockSpec(memory_space=pl.ANY),
                  pl.BlockSpec((1, d, fch), lambda w, f, e, a, c: (e[w], 0, fblk(w, f, c))),
                  pl.BlockSpec((1, 1, fch), lambda w, f, e, a, c: (e[w], 0, fblk(w, f, c))),
                  pl.BlockSpec((1, d, fch), lambda w, f, e, a, c: (e[w], 0, fblk(w, f, c))),
                  pl.BlockSpec((1, 1, fch), lambda w, f, e, a, c: (e[w], 0, fblk(w, f, c))),
                  pl.BlockSpec((1, fch, d), lambda w, f, e, a, c: (e[w], fblk(w, f, c), 0)),
                  pl.BlockSpec((1, 1, d), lambda w, f, e, a, c: (e[w], 0, 0))],
        out_specs=pl.BlockSpec(memory_space=pl.ANY),
        scratch_shapes=[pltpu.VMEM((2, cap, d), BF16), pltpu.VMEM((cap, d), F32), pltpu.VMEM((d, fch), BF16),
                        pltpu.VMEM((d, fch), BF16), pltpu.VMEM((fch, d), BF16), pltpu.SemaphoreType.DMA((3,))],
    )
    return pl.pallas_call(
        _expert_body,
        grid_spec=grid_spec,
        out_shape=jax.ShapeDtypeStruct((n_rows, d), F32),
        compiler_params=_params(("arbitrary", "arbitrary")),
        name="moe_experts",
    )(e_w, first, cnt, xg, w_gate, b_gate.reshape(n_exp, 1, de), w_up, b_up.reshape(n_exp, 1, de), w_down,
      b_down.reshape(n_exp, 1, d))


def _combine_body(pos_ref, h_ref, tg_ref, eo_hbm, y0_ref, y1_ref, rows, sem, *, tiles0):
    tile = pl.program_id(0)
    tm = h_ref.shape[0]
    base = tile * tm

    def row_copy(n, k, r):
        return pltpu.make_async_copy(eo_hbm.at[pl.ds(r, 1), :], rows.at[k, pl.ds(n, 1), :], sem.at[0])

    def issue(n, c):
        for k in range(TOP_K):
            row_copy(n, k, pos_ref[(base + n) * TOP_K + k]).start(priority=k % 2)
        return c

    def drain(n, c):
        for k in range(TOP_K):
            row_copy(n, k, 0).wait()
        return c

    lax.fori_loop(0, tm, issue, 0)
    lax.fori_loop(0, tm, drain, 0)
    tg = tg_ref[...]
    out = tg[:, 0:1] * rows[0]
    for k in range(1, TOP_K):
        out = out + tg[:, k:k + 1] * rows[k]
    y = h_ref[...] + out

    @pl.when(tile < tiles0)
    def _():
        y0_ref[...] = y

    @pl.when(tile >= tiles0)
    def _():
        y1_ref[...] = y


def _combine(h, tg, pos, eo, n0):
    n, d = h.shape
    tm = math.gcd(math.gcd(COMBINE_TILE, n0), n - n0)
    tiles0 = n0 // tm
    grid_spec = pltpu.PrefetchScalarGridSpec(
        num_scalar_prefetch=1,
        grid=(n // tm,),
        in_specs=[pl.BlockSpec((tm, d), lambda i, p: (i, 0)), pl.BlockSpec((tm, TOP_K), lambda i, p: (i, 0)),
                  pl.BlockSpec(memory_space=pl.ANY)],
        out_specs=[pl.BlockSpec((tm, d), lambda i, p: (jnp.minimum(i, tiles0 - 1), 0)),
                   pl.BlockSpec((tm, d), lambda i, p: (jnp.maximum(i - tiles0, 0), 0))],
        scratch_shapes=[pltpu.VMEM((TOP_K, tm, d), F32), pltpu.SemaphoreType.DMA((1,))],
    )
    return pl.pallas_call(
        functools.partial(_combine_body, tiles0=tiles0),
        grid_spec=grid_spec,
        out_shape=[jax.ShapeDtypeStruct((n0, d), F32), jax.ShapeDtypeStruct((n - n0, d), F32)],
        compiler_params=_params(("arbitrary",)),
        name="moe_combine",
    )(pos, h, tg, eo)


def _moe(h, xn, ti, tg, n0, w_gate, b_gate, w_up, b_up, w_down, b_down):
    n_exp = w_gate.shape[0]
    n_rows, pos, row_token, n_used, e_w, first, cnt = _moe_plan(ti, n_exp)
    xg = _dispatch(xn, row_token, n_used, n_rows)
    eo = _experts(xg, e_w, first, cnt, w_gate, b_gate, w_up, b_up, w_down, b_down)
    return _combine(h, tg, pos, eo, n0)


def _rope_tables(pos, rope, reps):
    half = rope // 2
    inv_freq = jnp.power(ROPE_THETA, -jnp.arange(half, dtype=F32) / half)
    ang = pos.astype(F32)[:, None] * inv_freq[None, :]
    cos, sin = jnp.cos(ang), jnp.sin(ang)
    return (jnp.tile(jnp.concatenate([cos, cos], axis=1), (1, reps)),
            jnp.tile(jnp.concatenate([-sin, sin], axis=1), (1, reps)))


def kernel(x_prompt, x_sample, cache_ckv, cache_kpe, cache_kscale, state_conv, state_h, page_table,
           norm_attn_gain, w_in, conv_w, conv_b, lru_w_a, lru_b_a, lru_w_i, lru_b_i, lru_lambda,
           q_norm_gain, w_uq, kv_norm_gain, w_uk, w_uv, qk_gain_q, qk_gain_k,
           out_norm_rg, out_norm_mla, w_out, norm_ffn_gain, w_router, b_router,
           w_gate, b_gate, w_up, b_up, w_down, b_down):
    nb, seq, d = x_prompt.shape
    db, t, _ = x_sample.shape
    assert t == SUBLANES, "the sample group is laid out one sequence per sublane group"
    lru_w = state_h.shape[1]
    kv_lora, nh, _ = w_uk.shape
    q_lora = w_uq.shape[0]
    rope = cache_kpe.shape[2]
    qk_head = QK_NOPE + rope
    n_p, n_s = nb * seq, db * t
    past_len = page_table.shape[1] * cache_ckv.shape[1]

    x_all = jnp.concatenate([x_prompt.reshape(n_p, d), x_sample.reshape(n_s, d)], axis=0)
    pos = jnp.concatenate([jnp.tile(jnp.arange(seq), nb), jnp.tile(past_len + jnp.arange(t), db)])
    cos_w, sin_w = _rope_tables(pos, rope, nh)

    xrg, gate, ql, kvl, kr = _in_proj(x_all, norm_attn_gain, w_in.astype(BF16),
                                      (lru_w, lru_w, q_lora, kv_lora, rope))

    w_uq3 = w_uq.reshape(q_lora, nh, qk_head)
    wqn_bf = w_uq3[:, :, :QK_NOPE].reshape(q_lora, nh * QK_NOPE).astype(BF16)
    wqr_bf = w_uq3[:, :, QK_NOPE:].reshape(q_lora, nh * rope).astype(BF16)
    wukt_bf = jnp.transpose(w_uk, (1, 2, 0)).astype(BF16)
    wuk_bf = w_uk.reshape(kv_lora, nh * QK_NOPE).astype(BF16)
    qabs, qpe, c, kpe, ks = _mla_proj(
        ql, kvl, kr, cos_w, sin_w, q_norm_gain, wqn_bf, wqr_bf, qk_gain_q[:QK_NOPE],
        jnp.tile(qk_gain_q[QK_NOPE:], nh), qk_gain_k[:QK_NOPE], qk_gain_k[QK_NOPE:], kv_norm_gain,
        wukt_bf, wuk_bf, nh)

    wai_bf = jnp.concatenate([lru_w_a, lru_w_i], axis=-1).astype(BF16)
    y_p, h_p = _rec_prompt(xrg, gate, nb, seq, conv_w, conv_b, wai_bf, lru_b_a, lru_b_i, lru_lambda)
    xrg_s = xrg[n_p:].reshape(db, t, lru_w)
    xpad = jnp.concatenate([state_conv, xrg_s], axis=1)
    x_shift = [xpad[:, k:k + t].reshape(n_s, lru_w) for k in range(CONV_W)]
    y_s, h_s = _rec_sample(x_shift, gate, n_p, state_h, conv_w, conv_b, wai_bf, lru_b_a, lru_b_i, lru_lambda)
    p_conv = xrg[:n_p].reshape(nb, seq, lru_w)[:, seq - (CONV_W - 1):]
    s_conv = xpad[:, t:]

    kst = jnp.transpose(ks[:n_p])
    olat = _prompt_attention(qabs, qpe, c, kpe, kst, nb, seq)
    kst_new = jnp.swapaxes(ks[n_p:].reshape(db, t, nh), 1, 2)
    olat = _sample_attention(olat, qabs, qpe, c, kpe, kst_new, n_p, db, page_table, cache_ckv,
                             jnp.swapaxes(cache_kpe, 1, 2), jnp.swapaxes(cache_kscale, 1, 2))

    wuv_bf = jnp.transpose(w_uv, (1, 0, 2)).astype(BF16)
    hh, xn, ti, tg = _merge_route(x_all, jnp.concatenate([y_p, y_s], axis=0), olat,
                                  wuv_bf, out_norm_rg, out_norm_mla, w_out.astype(BF16), norm_ffn_gain,
                                  w_router.astype(BF16), b_router)
    y_p2, y_s2 = _moe(hh, xn, ti, tg, n_p, w_gate, b_gate, w_up, b_up, w_down, b_down)

    return (y_p2.reshape(nb, seq, d), y_s2.reshape(db, t, d),
            c[:n_p].reshape(nb, seq, kv_lora), kpe[:n_p].reshape(nb, seq, rope), ks[:n_p].reshape(nb, seq, nh),
            p_conv, h_p.reshape(nb, lru_w),
            c[n_p:].reshape(db, t, kv_lora), kpe[n_p:].reshape(db, t, rope), ks[n_p:].reshape(db, t, nh),
            s_conv, h_s)
```

```python
import functools
import math

import jax
import jax.numpy as jnp
from jax import lax
from jax.experimental import pallas as pl
from jax.experimental.pallas import tpu as pltpu

F32 = jnp.float32
BF16 = jnp.bfloat16
I32 = jnp.int32

EPS = 1e-6
LRU_C = 8.0
LRU_BLOCK_W = 128
CONV_W = 4
ROPE_THETA = 10000.0
QK_NOPE = 128
TOP_K = 4
SWIGLU_LIMIT = 7.0
SWIGLU_ALPHA = 1.702

SUBLANES = 8
LANES = 128
VMEM_LIMIT_BYTES = 56 * 1024 * 1024

TOKEN_TILE = 256
SCAN_TILE = 256
ATTN_TQ = 256
ATTN_TK = 512
PAGES_PER_GROUP = 16
PAGES_PER_SOFTMAX = 16
PAGE_SLOTS = 3
MOE_SUB = 256
MOE_CAP_SUB = 6
MOE_BLOCK_SUBS = 4
MOE_F = 256
MOE_DOWN_COLS = 512
DMA_UNROLL = 8
COMBINE_TILE = 128


def _params(sem, vmem=VMEM_LIMIT_BYTES):
    return pltpu.CompilerParams(dimension_semantics=sem, vmem_limit_bytes=vmem)


def _dot(a, b):
    return jnp.dot(a, b, preferred_element_type=F32)


def _dot_nt(a, b):
    return lax.dot_general(a, b, (((1,), (1,)), ((), ())), preferred_element_type=F32)


def _rms(x):
    return x * lax.rsqrt(jnp.mean(x * x, axis=-1, keepdims=True) + EPS)


def _const_spec(shape):
    nd = len(shape)
    return pl.BlockSpec(shape, lambda *_: (0,) * nd)


def _two_part_specs(tm, width, tiles0):
    return [pl.BlockSpec((tm, width), lambda i: (jnp.minimum(i, tiles0 - 1), 0)),
            pl.BlockSpec((tm, width), lambda i: (jnp.maximum(i - tiles0, 0), 0))]


def _two_part_tile(a_ref, b_ref, tiles0):
    return jnp.where(pl.program_id(0) < tiles0, a_ref[...], b_ref[...])


def _inproj_body(x0_ref, x1_ref, g_ref, w_ref, *out_refs, offs, tiles0):
    xn = (_rms(_two_part_tile(x0_ref, x1_ref, tiles0)) * g_ref[...]).astype(BF16)
    for o_ref, (a, b) in zip(out_refs, offs):
        o_ref[...] = _dot(xn, w_ref[:, a:b])


def _in_proj(x0, x1, gain, w_bf, widths):
    n0, d = x0.shape
    n = n0 + x1.shape[0]
    tm = math.gcd(math.gcd(TOKEN_TILE, n0), x1.shape[0])
    tiles0 = n0 // tm
    offs, a = [], 0
    for w in widths:
        offs.append((a, a + w))
        a += w
    return pl.pallas_call(
        functools.partial(_inproj_body, offs=tuple(offs), tiles0=tiles0),
        grid=(n // tm,),
        in_specs=_two_part_specs(tm, d, tiles0) + [_const_spec((1, d)), _const_spec(w_bf.shape)],
        out_specs=[pl.BlockSpec((tm, w), lambda i: (i, 0)) for w in widths],
        out_shape=[jax.ShapeDtypeStruct((n, w), F32) for w in widths],
        compiler_params=_params(("parallel",)),
        name="in_proj",
    )(x0, x1, gain.reshape(1, d), w_bf)


def _mla_proj_body(ql_ref, kvl_ref, kr_ref, cos_ref, sin_ref, qng_ref, wqn_ref, wqr_ref, gqn_ref, gqr_ref,
                   gkn_ref, gkr_ref, kvg_ref, wukt_ref, wuk_ref,
                   qabs_ref, qpe_ref, c_ref, kpe_ref, ks_ref, *, nh):
    tm = ql_ref.shape[0]
    rope = kr_ref.shape[1]
    half = rope // 2
    qk_head = QK_NOPE + rope
    g8 = tm // SUBLANES

    qn = (_rms(ql_ref[...]) * qng_ref[...]).astype(BF16)
    qnope = _dot(qn, wqn_ref[...])
    qrope = _dot(qn, wqr_ref[...])
    r2 = qrope * qrope
    lane = lax.broadcasted_iota(I32, (tm, LANES), 1)
    heads_per_vreg = LANES // rope
    inv_heads = []
    for h in range(nh):
        blk = qnope[:, h * QK_NOPE:(h + 1) * QK_NOPE]
        ssn = jnp.sum(blk * blk, axis=-1, keepdims=True)
        v, sub = divmod(h, heads_per_vreg)
        rblk = r2[:, v * LANES:(v + 1) * LANES]
        in_head = (lane >= sub * rope) & (lane < (sub + 1) * rope)
        ssr = jnp.sum(jnp.where(in_head, rblk, 0.0), axis=-1, keepdims=True)
        inv_h = lax.rsqrt((ssn + ssr) / qk_head + EPS)
        inv_heads.append(inv_h)
        qa_in = ((blk * inv_h) * gqn_ref[...]) * gkn_ref[...]
        qabs = _dot(qa_in.astype(BF16), wukt_ref[h])
        qabs_ref[:, h] = qabs.reshape(g8, SUBLANES, qabs.shape[1])

    cols = []
    for v in range(nh // heads_per_vreg):
        col = inv_heads[v * heads_per_vreg]
        for sub in range(1, heads_per_vreg):
            col = jnp.where(lane >= sub * rope, inv_heads[v * heads_per_vreg + sub], col)
        cols.append(col)
    inv_all = jnp.concatenate(cols, axis=1)
    qr = (qrope * inv_all) * gqr_ref[...]
    wide = lax.broadcasted_iota(I32, qr.shape, 1)
    first_half = (wide % rope) < half
    width = qr.shape[1]
    rot = jnp.where(first_half, pltpu.roll(qr, width - half, 1), pltpu.roll(qr, half, 1))
    qpe = qr * cos_ref[...] + rot * sin_ref[...]
    for h in range(nh):
        qpe_ref[:, h] = qpe[:, h * rope:(h + 1) * rope].reshape(g8, SUBLANES, rope)

    c = _rms(kvl_ref[...]) * kvg_ref[...]
    c_ref[...] = c
    kn = _dot(c.astype(BF16), wuk_ref[...])
    kr = kr_ref[...]
    sskr = jnp.sum(kr * kr, axis=-1, keepdims=True)
    lane_h = lax.broadcasted_iota(I32, (tm, nh), 1)
    ks = jnp.zeros((tm, nh), F32)
    for h in range(nh):
        blk = kn[:, h * QK_NOPE:(h + 1) * QK_NOPE]
        ssk = jnp.sum(blk * blk, axis=-1, keepdims=True)
        ks = jnp.where(lane_h == h, lax.rsqrt((ssk + sskr) / qk_head + EPS), ks)
    ks_ref[...] = ks
    krg = kr * gkr_ref[...]
    swapped = jnp.concatenate([krg[:, half:], krg[:, :half]], axis=1)
    kpe_ref[...] = krg * cos_ref[:, :rope] + swapped * sin_ref[:, :rope]


def _mla_proj(ql, kvl, kr, cos_w, sin_w, q_norm_gain, wqn_bf, wqr_bf, gqn, gqr_w, gkn, gkr, kv_norm_gain,
              wukt_bf, wuk_bf, nh):
    n, q_lora = ql.shape
    kv_lora = kvl.shape[1]
    rope = kr.shape[1]
    tm = min(TOKEN_TILE, n)
    g8 = tm // SUBLANES
    row = lambda i: (i, 0)
    row4 = lambda i: (i, 0, 0, 0)
    return pl.pallas_call(
        functools.partial(_mla_proj_body, nh=nh),
        grid=(n // tm,),
        in_specs=[pl.BlockSpec((tm, q_lora), row), pl.BlockSpec((tm, kv_lora), row), pl.BlockSpec((tm, rope), row),
                  pl.BlockSpec((tm, nh * rope), row), pl.BlockSpec((tm, nh * rope), row),
                  _const_spec((1, q_lora)), _const_spec(wqn_bf.shape), _const_spec(wqr_bf.shape),
                  _const_spec((1, QK_NOPE)), _const_spec((1, nh * rope)), _const_spec((1, QK_NOPE)),
                  _const_spec((1, rope)), _const_spec((1, kv_lora)), _const_spec(wukt_bf.shape),
                  _const_spec(wuk_bf.shape)],
        out_specs=[pl.BlockSpec((g8, nh, SUBLANES, kv_lora), row4), pl.BlockSpec((g8, nh, SUBLANES, rope), row4),
                   pl.BlockSpec((tm, kv_lora), row), pl.BlockSpec((tm, rope), row), pl.BlockSpec((tm, nh), row)],
        out_shape=[jax.ShapeDtypeStruct((n // SUBLANES, nh, SUBLANES, kv_lora), F32),
                   jax.ShapeDtypeStruct((n // SUBLANES, nh, SUBLANES, rope), F32),
                   jax.ShapeDtypeStruct((n, kv_lora), F32), jax.ShapeDtypeStruct((n, rope), F32),
                   jax.ShapeDtypeStruct((n, nh), F32)],
        compiler_params=_params(("parallel",)),
        name="mla_proj",
    )(ql, kvl, kr, cos_w, sin_w, q_norm_gain.reshape(1, -1), wqn_bf, wqr_bf, gqn.reshape(1, -1),
      gqr_w.reshape(1, -1), gkn.reshape(1, -1), gkr.reshape(1, -1), kv_norm_gain.reshape(1, -1), wukt_bf, wuk_bf)


def _expm1(x):
    u = jnp.exp(x)
    um1 = u - 1.0
    return jnp.where(u == 1.0, x, jnp.where(um1 == -1.0, -1.0, um1 * x / jnp.log(u)))


def _lru_gates(xc, wai_ref, ba_ref, bi_ref, lam_ref):
    nblk = wai_ref.shape[0]
    xcb = xc.astype(BF16)
    ra, ri = [], []
    for n in range(nblk):
        g = _dot(xcb[:, n * LRU_BLOCK_W:(n + 1) * LRU_BLOCK_W], wai_ref[n])
        ra.append(g[:, :LRU_BLOCK_W])
        ri.append(g[:, LRU_BLOCK_W:])
    r = jax.nn.sigmoid(jnp.concatenate(ra, axis=1) + ba_ref[...])
    i = jax.nn.sigmoid(jnp.concatenate(ri, axis=1) + bi_ref[...])
    log_a = (-LRU_C * r) * jax.nn.softplus(-lam_ref[...])
    a = jnp.exp(log_a)
    mult = jnp.sqrt(jnp.maximum(-_expm1(2.0 * log_a), 0.0))
    return a, mult * i * xc


def _group_scan(a, u, rows):
    for s in (1, 2, 4):
        keep = rows >= s
        a_sh = jnp.where(keep, pltpu.roll(a, s, 0), 1.0)
        u_sh = jnp.where(keep, pltpu.roll(u, s, 0), 0.0)
        u = a * u_sh + u
        a = a * a_sh
    return a, u


def _rec_prompt_body(x_ref, gate_ref, cw_ref, cb_ref, wai_ref, ba_ref, bi_ref, lam_ref, y_ref, ht_ref,
                     xbuf, a_s, u_s, h_s, *, tt):
    j = pl.program_id(1)
    w = x_ref.shape[1]

    @pl.when(j == 0)
    def _():
        xbuf[0:SUBLANES, :] = jnp.zeros((SUBLANES, w), F32)
        h_s[...] = jnp.zeros_like(h_s)

    xbuf[SUBLANES:SUBLANES + tt, :] = x_ref[...]
    xc = cb_ref[...]
    for k in range(CONV_W):
        lo = SUBLANES - (CONV_W - 1) + k
        xc = xc + cw_ref[k:k + 1, :] * xbuf[lo:lo + tt, :]
    xbuf[0:SUBLANES, :] = xbuf[tt:tt + SUBLANES, :]
    a, u = _lru_gates(xc, wai_ref, ba_ref, bi_ref, lam_ref)
    a_s[...] = a
    u_s[...] = u
    rows = lax.broadcasted_iota(I32, (SUBLANES, w), 0)

    def grp(g, h):
        r0 = pl.multiple_of(g * SUBLANES, SUBLANES)
        ag, ug = _group_scan(a_s[pl.ds(r0, SUBLANES), :], u_s[pl.ds(r0, SUBLANES), :], rows)
        hs = ug + ag * h
        y_ref[pl.ds(r0, SUBLANES), :] = jax.nn.gelu(gate_ref[pl.ds(r0, SUBLANES), :]) * hs
        return hs[SUBLANES - 1:SUBLANES, :]

    h = lax.fori_loop(0, tt // SUBLANES, grp, h_s[...])
    h_s[...] = h
    ht_ref[0] = h


def _rec_prompt(xrg, gate, n_seq, seq_len, conv_w, conv_b, wai_bf, b_a, b_i, lam):
    w = xrg.shape[1]
    tt = min(SCAN_TILE, seq_len)
    nt = seq_len // tt
    row = lambda b, j: (b * nt + j, 0)
    vec = lambda a: a.reshape(1, w)
    return pl.pallas_call(
        functools.partial(_rec_prompt_body, tt=tt),
        grid=(n_seq, nt),
        in_specs=[pl.BlockSpec((tt, w), row), pl.BlockSpec((tt, w), row), _const_spec((CONV_W, w)),
                  _const_spec((1, w)), _const_spec(wai_bf.shape), _const_spec((1, w)), _const_spec((1, w)),
                  _const_spec((1, w))],
        out_specs=[pl.BlockSpec((tt, w), row), pl.BlockSpec((1, 1, w), lambda b, j: (b, 0, 0))],
        out_shape=[jax.ShapeDtypeStruct((n_seq * seq_len, w), F32), jax.ShapeDtypeStruct((n_seq, 1, w), F32)],
        scratch_shapes=[pltpu.VMEM((tt + SUBLANES, w), F32), pltpu.VMEM((tt, w), F32), pltpu.VMEM((tt, w), F32),
                        pltpu.VMEM((1, w), F32)],
        compiler_params=_params(("parallel", "arbitrary")),
        name="rglru_prompt",
    )(xrg, gate, conv_w, vec(conv_b), wai_bf, vec(b_a), vec(b_i), vec(lam))


def _rec_sample_body(x0_ref, x1_ref, x2_ref, x3_ref, gate_ref, h0_ref, cw_ref, cb_ref, wai_ref, ba_ref, bi_ref,
                     lam_ref, y_ref, ht_ref, a_s, u_s):
    tm, w = gate_ref.shape
    xc = cb_ref[...]
    for k, x_ref in enumerate((x0_ref, x1_ref, x2_ref, x3_ref)):
        xc = xc + cw_ref[k:k + 1, :] * x_ref[...]
    a, u = _lru_gates(xc, wai_ref, ba_ref, bi_ref, lam_ref)
    a_s[...] = a
    u_s[...] = u
    rows = lax.broadcasted_iota(I32, (SUBLANES, w), 0)

    def grp(g, carry):
        r0 = pl.multiple_of(g * SUBLANES, SUBLANES)
        ag, ug = _group_scan(a_s[pl.ds(r0, SUBLANES), :], u_s[pl.ds(r0, SUBLANES), :], rows)
        hs = ug + ag * h0_ref[pl.ds(g, 1), :]
        y_ref[pl.ds(r0, SUBLANES), :] = jax.nn.gelu(gate_ref[pl.ds(r0, SUBLANES), :]) * hs
        ht_ref[pl.ds(g, 1), :] = hs[SUBLANES - 1:SUBLANES, :]
        return carry

    lax.fori_loop(0, tm // SUBLANES, grp, 0)


def _rec_sample(x_shift, gate, gate_row0, h0, conv_w, conv_b, wai_bf, b_a, b_i, lam):
    n, w = x_shift[0].shape
    tm = min(TOKEN_TILE, n)
    g8 = tm // SUBLANES
    g0 = gate_row0 // tm
    row = lambda i: (i, 0)
    vec = lambda a: a.reshape(1, w)
    return pl.pallas_call(
        _rec_sample_body,
        grid=(n // tm,),
        in_specs=[pl.BlockSpec((tm, w), row)] * CONV_W
        + [pl.BlockSpec((tm, w), lambda i: (g0 + i, 0)), pl.BlockSpec((g8, w), row), _const_spec((CONV_W, w)),
           _const_spec((1, w)), _const_spec(wai_bf.shape), _const_spec((1, w)), _const_spec((1, w)),
           _const_spec((1, w))],
        out_specs=[pl.BlockSpec((tm, w), row), pl.BlockSpec((g8, w), row)],
        out_shape=[jax.ShapeDtypeStruct((n, w), F32), jax.ShapeDtypeStruct((n // SUBLANES, w), F32)],
        scratch_shapes=[pltpu.VMEM((tm, w), F32), pltpu.VMEM((tm, w), F32)],
        compiler_params=_params(("parallel",)),
        name="rglru_sample",
    )(*x_shift, gate, h0, conv_w, vec(conv_b), wai_bf, vec(b_a), vec(b_i), vec(lam))


def _fold_lanes(x, op):
    n = x.shape[1]
    if n % LANES != 0 or n == LANES:
        return x
    out = x[:, :LANES]
    for c0 in range(LANES, n, LANES):
        out = op(out, x[:, c0:c0 + LANES])
    return out


def _softmax_step(s, m, l, acc, cb):
    m_new = jnp.maximum(m, jnp.max(_fold_lanes(s, jnp.maximum), axis=-1, keepdims=True))
    corr = jnp.exp2(m - m_new)
    p = jnp.exp2(s - m_new)
    l = l * corr + jnp.sum(_fold_lanes(p, jnp.add), axis=-1, keepdims=True)
    acc = acc * corr + _dot(p.astype(BF16), cb)
    return m_new, l, acc


def _pattn_body(qa_ref, qp_ref, c_ref, kp_ref, kst_ref, o_ref, qa_s, qp_s, m_s, l_s, acc_s, *, tq, tk, scale):
    i = pl.program_id(1)
    j = pl.program_id(2)
    nh = qa_s.shape[0]
    kv = qa_s.shape[2]
    rope = qp_s.shape[2]

    @pl.when(j == 0)
    def _():
        for h in range(nh):
            qa_s[h] = qa_ref[:, h].reshape(tq, kv).astype(BF16)
            qp_s[h] = qp_ref[:, h].reshape(tq, rope).astype(BF16)
        m_s[...] = jnp.full_like(m_s, -jnp.inf)
        l_s[...] = jnp.zeros_like(l_s)
        acc_s[...] = jnp.zeros_like(acc_s)

    def kv_tile(masked):
        cb = c_ref[...].astype(BF16)
        kb = kp_ref[...].astype(BF16)
        key_scale = kst_ref[...] * scale
        if masked:
            qpos = i * tq + lax.broadcasted_iota(I32, (tq, tk), 0)
            kpos = j * tk + lax.broadcasted_iota(I32, (tq, tk), 1)
            visible = kpos <= qpos

        def logits(h):
            return _dot_nt(qa_s[h], cb) + _dot_nt(qp_s[h], kb)

        qk_next = logits(0)
        for h in range(nh):
            s = qk_next * key_scale[h:h + 1, :]
            if h + 1 < nh:
                qk_next = logits(h + 1)
            if masked:
                s = jnp.where(visible, s, -jnp.inf)
            m_s[h], l_s[h], acc_s[h] = _softmax_step(s, m_s[h], l_s[h], acc_s[h], cb)

    active = j * tk < (i + 1) * tq
    crosses_diagonal = (j + 1) * tk - 1 > i * tq

    @pl.when(active & crosses_diagonal)
    def _():
        kv_tile(True)

    @pl.when(active & jnp.logical_not(crosses_diagonal))
    def _():
        kv_tile(False)

    @pl.when(j == pl.num_programs(2) - 1)
    def _():
        for h in range(nh):
            o_ref[:, h] = (acc_s[h] / l_s[h]).reshape(tq // SUBLANES, SUBLANES, kv)


def _prompt_attention(qabs, qpe, c, kpe, kst, n_seq, seq_len):
    n_groups, nh, _, kv = qabs.shape
    rope = qpe.shape[3]
    tq = min(ATTN_TQ, seq_len)
    tk = min(ATTN_TK, seq_len)
    nq, nk = seq_len // tq, seq_len // tk
    g8 = tq // SUBLANES
    scale = math.log2(math.e) / math.sqrt(QK_NOPE + rope)

    def kblk(i, j):
        return jnp.minimum(j, ((i + 1) * tq - 1) // tk)

    return pl.pallas_call(
        functools.partial(_pattn_body, tq=tq, tk=tk, scale=scale),
        grid=(n_seq, nq, nk),
        in_specs=[pl.BlockSpec((g8, nh, SUBLANES, kv), lambda b, i, j: (b * nq + i, 0, 0, 0)),
                  pl.BlockSpec((g8, nh, SUBLANES, rope), lambda b, i, j: (b * nq + i, 0, 0, 0)),
                  pl.BlockSpec((tk, kv), lambda b, i, j: (b * nk + kblk(i, j), 0)),
                  pl.BlockSpec((tk, rope), lambda b, i, j: (b * nk + kblk(i, j), 0)),
                  pl.BlockSpec((nh, tk), lambda b, i, j: (0, b * nk + kblk(i, j)))],
        out_specs=pl.BlockSpec((g8, nh, SUBLANES, kv), lambda b, i, j: (b * nq + i, 0, 0, 0)),
        out_shape=jax.ShapeDtypeStruct((n_groups, nh, SUBLANES, kv), F32),
        scratch_shapes=[pltpu.VMEM((nh, tq, kv), BF16), pltpu.VMEM((nh, tq, rope), BF16),
                        pltpu.VMEM((nh, tq, 1), F32), pltpu.VMEM((nh, tq, 1), F32), pltpu.VMEM((nh, tq, kv), F32)],
        compiler_params=_params(("parallel", "parallel", "arbitrary")),
        name="mla_prompt_attention",
    )(qabs, qpe, c, kpe, kst)


def _rows_per_head(x, t):
    nh, n = x.shape
    return jnp.concatenate([jnp.broadcast_to(x[h:h + 1, :], (t, n)) for h in range(nh)], axis=0)


def _sattn_body(pt_ref, qa_ref, qp_ref, cn_ref, kn_ref, snt_ref, ckv_hbm, kpet_hbm, kst_hbm, olat_hbm, o_ref,
                cbuf, kbuf, sbuf, sem, m_s, l_s, acc_s, *, n_pages, group, sub_pages, scale):
    del olat_hbm
    b = pl.program_id(0)
    nb = pl.num_programs(0)
    _, nh, t, kv = qa_ref.shape
    rope = qp_ref.shape[3]
    page = cbuf.shape[2]
    ng = n_pages // group
    m_rows = nh * t

    def copies(bb, g, slot, p):
        phys = pt_ref[bb * n_pages + g * group + p]
        return (pltpu.make_async_copy(ckv_hbm.at[phys], cbuf.at[slot, p], sem.at[slot, 0]),
                pltpu.make_async_copy(kpet_hbm.at[phys], kbuf.at[slot, p], sem.at[slot, 1]),
                pltpu.make_async_copy(kst_hbm.at[phys], sbuf.at[slot, p], sem.at[slot, 2]))

    def start_group(bb, g, slot):
        for p in range(group):
            for cp in copies(bb, g, slot, p):
                cp.start(priority=p % 2)

    def wait_group(slot):
        for p in range(group):
            for cp in copies(0, 0, slot, p):
                cp.wait()

    last = nb * ng - 1
    n_slots = cbuf.shape[0]
    ahead = n_slots - 1

    def start_step(step, slot):
        step = jnp.minimum(step, last)
        start_group(step // ng, step % ng, slot)

    @pl.when(b == 0)
    def _():
        for k in range(ahead):
            start_step(k, k)

    qa = qa_ref[0].reshape(m_rows, kv).astype(BF16)
    qp = qp_ref[0].reshape(m_rows, rope).astype(BF16)
    m_s[...] = jnp.full_like(m_s, -jnp.inf)
    l_s[...] = jnp.zeros_like(l_s)
    acc_s[...] = jnp.zeros_like(acc_s)

    def grp(g, carry):
        step = b * ng + g
        slot = step % n_slots
        start_step(step + ahead, (step + ahead) % n_slots)
        wait_group(slot)
        state = (m_s[...], l_s[...], acc_s[...])
        for p0 in range(0, group, sub_pages):
            pages = range(p0, p0 + sub_pages)
            cb = cbuf[slot, p0:p0 + sub_pages].reshape(sub_pages * page, kv).astype(BF16)
            s_rope = jnp.concatenate([_dot(qp, kbuf[slot, p].astype(BF16)) for p in pages], axis=1)
            ksx = jnp.concatenate([_rows_per_head(sbuf[slot, p], t) for p in pages], axis=1)
            s = ((_dot_nt(qa, cb) + s_rope) * ksx) * scale
            state = _softmax_step(s, *state, cb)
        m_s[...], l_s[...], acc_s[...] = state
        return carry

    lax.fori_loop(0, ng, grp, 0)

    @pl.when(b == nb - 1)
    def _():
        for k in range(1, ahead + 1):
            wait_group((last + k) % n_slots)

    cnb = cn_ref[...].astype(BF16)
    s = ((_dot_nt(qa, cnb) + _dot_nt(qp, kn_ref[...].astype(BF16))) * _rows_per_head(snt_ref[0], t)) * scale
    q_t = lax.broadcasted_iota(I32, (m_rows, t), 0) % t
    k_t = lax.broadcasted_iota(I32, (m_rows, t), 1)
    _, l, acc = _softmax_step(jnp.where(k_t <= q_t, s, -jnp.inf), m_s[...], l_s[...], acc_s[...], cnb)
    o_ref[0] = (acc / l).reshape(nh, t, kv)


def _sample_attention(olat, qabs, qpe, c, kpe, kst_new, row0, n_seq, page_table, cache_ckv, cache_kpet, cache_kst):
    _, nh, t, kv = qabs.shape
    rope = qpe.shape[3]
    n_pages = page_table.shape[1]
    page = cache_ckv.shape[1]
    group = math.gcd(PAGES_PER_GROUP, n_pages)
    g0 = row0 // t
    scale = math.log2(math.e) / math.sqrt(QK_NOPE + rope)
    any_spec = pl.BlockSpec(memory_space=pl.ANY)
    grid_spec = pltpu.PrefetchScalarGridSpec(
        num_scalar_prefetch=1,
        grid=(n_seq,),
        in_specs=[pl.BlockSpec((1, nh, t, kv), lambda b, pt: (g0 + b, 0, 0, 0)),
                  pl.BlockSpec((1, nh, t, rope), lambda b, pt: (g0 + b, 0, 0, 0)),
                  pl.BlockSpec((t, kv), lambda b, pt: (g0 + b, 0)),
                  pl.BlockSpec((t, rope), lambda b, pt: (g0 + b, 0)),
                  pl.BlockSpec((1, nh, t), lambda b, pt: (b, 0, 0)),
                  any_spec, any_spec, any_spec, any_spec],
        out_specs=pl.BlockSpec((1, nh, t, kv), lambda b, pt: (g0 + b, 0, 0, 0)),
        scratch_shapes=[pltpu.VMEM((PAGE_SLOTS, group, page, kv), F32),
                        pltpu.VMEM((PAGE_SLOTS, group, rope, page), F32),
                        pltpu.VMEM((PAGE_SLOTS, group, nh, page), F32), pltpu.SemaphoreType.DMA((PAGE_SLOTS, 3)),
                        pltpu.VMEM((nh * t, 1), F32), pltpu.VMEM((nh * t, 1), F32), pltpu.VMEM((nh * t, kv), F32)],
    )
    return pl.pallas_call(
        functools.partial(_sattn_body, n_pages=n_pages, group=group, sub_pages=math.gcd(PAGES_PER_SOFTMAX, group),
                          scale=scale),
        grid_spec=grid_spec,
        out_shape=jax.ShapeDtypeStruct(olat.shape, F32),
        input_output_aliases={9: 0},
        compiler_params=_params(("arbitrary",)),
        name="mla_sample_attention",
    )(page_table.reshape(-1), qabs, qpe, c, kpe, kst_new, cache_ckv, cache_kpet, cache_kst, olat)


def _merge_body(x0_ref, x1_ref, yrg0_ref, yrg1_ref, ol_ref, wuv_ref, g1_ref, g2_ref, wout_ref, gf_ref, wr_ref,
                br_ref, h_ref, xn_ref, ti_ref, tg_ref, *, tiles0):
    tm = x0_ref.shape[0]
    _, nh, _, kv = ol_ref.shape
    w1 = yrg0_ref.shape[1]
    n_exp = wr_ref.shape[1]
    o = jnp.concatenate([_dot(ol_ref[:, h].reshape(tm, kv).astype(BF16), wuv_ref[h]) for h in range(nh)], axis=1)
    y1 = (_rms(_two_part_tile(yrg0_ref, yrg1_ref, tiles0)) * g1_ref[...]).astype(BF16)
    y2 = (_rms(o) * g2_ref[...]).astype(BF16)
    hh = _two_part_tile(x0_ref, x1_ref, tiles0) + (_dot(y1, wout_ref[:w1, :]) + _dot(y2, wout_ref[w1:, :]))
    h_ref[...] = hh
    xn = _rms(hh) * gf_ref[...]
    xn_ref[...] = xn
    logits = _dot(xn.astype(BF16), wr_ref[...]) + br_ref[...]
    lane = lax.broadcasted_iota(I32, (tm, n_exp), 1)
    vals, idxs = [], []
    for _ in range(TOP_K):
        m = jnp.max(logits, axis=-1, keepdims=True)
        idx = jnp.min(jnp.where(logits == m, lane, n_exp), axis=-1, keepdims=True)
        vals.append(m)
        idxs.append(idx)
        logits = jnp.where(lane == idx, -jnp.inf, logits)
    es = [jnp.exp(v - vals[0]) for v in vals]
    den = es[0]
    for e in es[1:]:
        den = den + e
    lane_k = lax.broadcasted_iota(I32, (tm, TOP_K), 1)
    ti = jnp.zeros((tm, TOP_K), I32)
    tg = jnp.zeros((tm, TOP_K), F32)
    for k in range(TOP_K):
        ti = jnp.where(lane_k == k, idxs[k], ti)
        tg = jnp.where(lane_k == k, es[k] / den, tg)
    ti_ref[...] = ti
    tg_ref[...] = tg


def _merge_route(x0, x1, yrg0, yrg1, olat, wuv_bf, g1, g2, wout_bf, gf, wr_bf, b_router):
    n0, d = x0.shape
    n = n0 + x1.shape[0]
    w1 = yrg0.shape[1]
    _, nh, _, kv = olat.shape
    w2 = wuv_bf.shape[0] * wuv_bf.shape[2]
    n_exp = wr_bf.shape[1]
    tm = math.gcd(math.gcd(TOKEN_TILE, n0), x1.shape[0])
    tiles0 = n0 // tm
    row = lambda i: (i, 0)
    return pl.pallas_call(
        functools.partial(_merge_body, tiles0=tiles0),
        grid=(n // tm,),
        in_specs=_two_part_specs(tm, d, tiles0) + _two_part_specs(tm, w1, tiles0)
        + [pl.BlockSpec((tm // SUBLANES, nh, SUBLANES, kv), lambda i: (i, 0, 0, 0)),
                  _const_spec(wuv_bf.shape), _const_spec((1, w1)), _const_spec((1, w2)), _const_spec(wout_bf.shape),
                  _const_spec((1, d)), _const_spec(wr_bf.shape), _const_spec((1, n_exp))],
        out_specs=[pl.BlockSpec((tm, d), row), pl.BlockSpec((tm, d), row), pl.BlockSpec((tm, TOP_K), row),
                   pl.BlockSpec((tm, TOP_K), row)],
        out_shape=[jax.ShapeDtypeStruct((n, d), F32), jax.ShapeDtypeStruct((n, d), F32),
                   jax.ShapeDtypeStruct((n, TOP_K), I32), jax.ShapeDtypeStruct((n, TOP_K), F32)],
        compiler_params=_params(("parallel",)),
        name="merge_route",
    )(x0, x1, yrg0, yrg1, olat, wuv_bf, g1.reshape(1, w1), g2.reshape(1, w2), wout_bf, gf.reshape(1, d), wr_bf,
      b_router.reshape(1, n_exp))


def _moe_plan(top_i, n_exp):
    n, k = top_i.shape
    n_sub = -(-n * k // MOE_SUB) + n_exp
    n_rows = n_sub * MOE_SUB
    n_wi = n_exp + n_sub // MOE_CAP_SUB
    flat = top_i.reshape(-1)
    onehot = (flat[:, None] == jnp.arange(n_exp, dtype=I32)[None, :]).astype(I32)
    csum = jnp.cumsum(onehot, axis=0)
    counts = csum[-1]
    rank = jnp.sum(onehot * csum, axis=1) - 1
    nsub_e = (counts + MOE_SUB - 1) // MOE_SUB
    sub_off = jnp.cumsum(nsub_e) - nsub_e
    pos = (sub_off * MOE_SUB)[flat] + rank
    row_token = jnp.zeros((n_rows,), I32).at[pos].set(jnp.arange(n * k, dtype=I32) // k, unique_indices=True)
    nwi_e = (nsub_e + MOE_CAP_SUB - 1) // MOE_CAP_SUB
    wi_end = jnp.cumsum(nwi_e)
    w = jnp.arange(n_wi, dtype=I32)
    e_w = jnp.minimum(jnp.sum((w[:, None] >= wi_end[None, :]).astype(I32), axis=1), n_exp - 1)
    local = w - (wi_end - nwi_e)[e_w]
    valid = w < wi_end[-1]
    first = jnp.where(valid, sub_off[e_w] + local * MOE_CAP_SUB, 0)
    cnt = jnp.where(valid, jnp.clip(nsub_e[e_w] - local * MOE_CAP_SUB, 0, MOE_CAP_SUB), 0)
    last_e = e_w[jnp.maximum(wi_end[-1] - 1, 0)]
    e_w = jnp.where(valid, e_w, last_e)
    n_used = jnp.sum(nsub_e).reshape(1)
    return (n_rows, pos.astype(I32), row_token, n_used.astype(I32), e_w.astype(I32), first.astype(I32),
            cnt.astype(I32))


def _dispatch_body(rt_ref, nu_ref, xn_hbm, o_ref, stage, sem):
    tile = pl.program_id(0)
    sub = stage.shape[1]
    n_used = nu_ref[0]
    slot = tile % 2

    def row_copy(half, r, tok):
        return pltpu.make_async_copy(xn_hbm.at[pl.ds(tok, 1), :], stage.at[half, pl.ds(r, 1), :], sem.at[half])

    def issue_tile(tt, half):
        base = tt * sub

        def issue(i, c):
            for u in range(DMA_UNROLL):
                r = i * DMA_UNROLL + u
                row_copy(half, r, rt_ref[base + r]).start(priority=u % 2)
            return c

        lax.fori_loop(0, sub // DMA_UNROLL, issue, 0)

    def drain(i, c):
        for u in range(DMA_UNROLL):
            row_copy(slot, i * DMA_UNROLL + u, 0).wait()
        return c

    @pl.when(tile == 0)
    def _():
        issue_tile(0, 0)

    @pl.when(tile + 1 < n_used)
    def _():
        issue_tile(tile + 1, 1 - slot)

    @pl.when(tile < n_used)
    def _():
        lax.fori_loop(0, sub // DMA_UNROLL, drain, 0)
        o_ref[...] = stage[slot].astype(o_ref.dtype)


def _dispatch(xn, row_token, n_used, n_rows):
    n, d = xn.shape
    grid_spec = pltpu.PrefetchScalarGridSpec(
        num_scalar_prefetch=2,
        grid=(n_rows // MOE_SUB,),
        in_specs=[pl.BlockSpec(memory_space=pl.ANY)],
        out_specs=pl.BlockSpec((MOE_SUB, d), lambda i, rt, nu: (jnp.minimum(i, nu[0] - 1), 0)),
        scratch_shapes=[pltpu.VMEM((2, MOE_SUB, d), xn.dtype), pltpu.SemaphoreType.DMA((2,))],
    )
    return pl.pallas_call(
        _dispatch_body,
        grid_spec=grid_spec,
        out_shape=jax.ShapeDtypeStruct((n_rows, d), BF16),
        compiler_params=_params(("arbitrary",)),
        name="moe_dispatch",
    )(row_token, n_used, xn)


def _expert_body(e_ref, first_ref, cnt_ref, xg_hbm, wg_ref, bg_ref, wu_ref, bu_ref, wd_ref, bd_ref, out_hbm,
                 xbuf, acc, wg_s, wu_s, wd_s, sem):
    w = pl.program_id(0)
    f = pl.program_id(1)
    nw = pl.num_programs(0)
    nf = pl.num_programs(1)
    cnt = cnt_ref[w]
    slot = w % 2
    sub = MOE_SUB
    d = acc.shape[1]

    def rows_in(item, half, s):
        return pltpu.make_async_copy(xg_hbm.at[pl.ds((first_ref[item] + s) * sub, sub), :],
                                     xbuf.at[half, pl.ds(s * sub, sub), :], sem.at[half])

    def rows_out(item, s):
        return pltpu.make_async_copy(acc.at[pl.ds(s * sub, sub), :],
                                     out_hbm.at[pl.ds((first_ref[item] + s) * sub, sub), :], sem.at[2])

    def each(n, fn):
        def body(s, c):
            fn(s)
            return c
        lax.fori_loop(0, n, body, 0)

    @pl.when(f == 0)
    def _():
        @pl.when(w == 0)
        def _():
            each(cnt, lambda s: rows_in(0, 0, s).start())

        each(cnt, lambda s: rows_in(w, slot, s).wait())

        @pl.when(w + 1 < nw)
        def _():
            each(cnt_ref[w + 1], lambda s: rows_in(w + 1, 1 - slot, s).start())

        @pl.when(w > 0)
        def _():
            each(cnt_ref[w - 1], lambda s: rows_out(w - 1, s).wait())

    @pl.when(cnt > 0)
    def _():
        wg_s[...] = wg_ref[0].astype(BF16)
        wu_s[...] = wu_ref[0].astype(BF16)
        wd_s[...] = wd_ref[0].astype(BF16)
        bg = bg_ref[0]
        bu = bu_ref[0]

        @pl.when(f == 0)
        def _():
            bd = jnp.broadcast_to(bd_ref[0], (sub, d))

            def init(s):
                acc[pl.ds(pl.multiple_of(s * sub, sub), sub), :] = bd

            each(cnt, init)

        def block(s0, nsub):
            rows = nsub * sub
            r0 = pl.multiple_of(s0 * sub, sub)
            xs = xbuf[slot, pl.ds(r0, rows), :]
            g = jnp.minimum(_dot(xs, wg_s[...]) + bg, SWIGLU_LIMIT)
            u = jnp.clip(_dot(xs, wu_s[...]) + bu, -SWIGLU_LIMIT, SWIGLU_LIMIT)
            act = ((u + 1.0) * g * jax.nn.sigmoid(SWIGLU_ALPHA * g)).astype(BF16)
            for c0 in range(0, d, MOE_DOWN_COLS):
                acc[pl.ds(r0, rows), c0:c0 + MOE_DOWN_COLS] += _dot(act, wd_s[:, c0:c0 + MOE_DOWN_COLS])

            @pl.when(f == nf - 1)
            def _():
                for i in range(nsub):
                    rows_out(w, s0 + i).start()

        n_big = cnt // MOE_BLOCK_SUBS
        each(n_big, lambda i: block(i * MOE_BLOCK_SUBS, MOE_BLOCK_SUBS))
        s0 = n_big * MOE_BLOCK_SUBS
        nsub = MOE_BLOCK_SUBS // 2
        while nsub >= 1:
            @pl.when((cnt & nsub) != 0)
            def _(s0=s0, nsub=nsub):
                block(s0, nsub)

            s0 = s0 + (cnt & nsub)
            nsub //= 2

    @pl.when((f == nf - 1) & (w == nw - 1))
    def _():
        each(cnt, lambda s: rows_out(w, s).wait())


def _experts(xg, e_w, first, cnt, w_gate, b_gate, w_up, b_up, w_down, b_down):
    n_rows, d = xg.shape
    n_exp, _, de = w_gate.shape
    n_wi = e_w.shape[0]
    fch = min(MOE_F, de)
    cap = MOE_SUB * MOE_CAP_SUB
    nf = de // fch

    def fblk(w, f, c):
        return jnp.where(c[w] > 0, f, nf - 1)

    grid_spec = pltpu.PrefetchScalarGridSpec(
        num_scalar_prefetch=3,
        grid=(n_wi, nf),
        in_specs=[pl.BlockSpec(memory_space=pl.ANY),
                  pl.BlockSpec((1, d, fch), lambda w, f, e, a, c: (e[w], 0, fblk(w, f, c))),
                  pl.BlockSpec((1, 1, fch), lambda w, f, e, a, c: (e[w], 0, fblk(w, f, c))),
                  pl.BlockSpec((1, d, fch), lambda w, f, e, a, c: (e[w], 0, fblk(w, f, c))),
                  pl.BlockSpec((1, 1, fch), lambda w, f, e, a, c: (e[w], 0, fblk(w, f, c))),
                  pl.BlockSpec((1, fch, d), lambda w, f, e, a, c: (e[w], fblk(w, f, c), 0)),
                  pl.BlockSpec((1, 1, d), lambda w, f, e, a, c: (e[w], 0, 0))],
        out_specs=pl.BlockSpec(memory_space=pl.ANY),
        scratch_shapes=[pltpu.VMEM((2, cap, d), BF16), pltpu.VMEM((cap, d), F32), pltpu.VMEM((d, fch), BF16),
                        pltpu.VMEM((d, fch), BF16), pltpu.VMEM((fch, d), BF16), pltpu.SemaphoreType.DMA((3,))],
    )
    return pl.pallas_call(
        _expert_body,
        grid_spec=grid_spec,
        out_shape=jax.ShapeDtypeStruct((n_rows, d), F32),
        compiler_params=_params(("arbitrary", "arbitrary")),
        name="moe_experts",
    )(e_w, first, cnt, xg, w_gate, b_gate.reshape(n_exp, 1, de), w_up, b_up.reshape(n_exp, 1, de), w_down,
      b_down.reshape(n_exp, 1, d))


def _combine_body(pos_ref, h_ref, tg_ref, eo_hbm, y0_ref, y1_ref, rows, sem, *, tiles0):
    tile = pl.program_id(0)
    tm = h_ref.shape[0]
    slot = tile % 2

    def row_copy(half, n, k, r):
        return pltpu.make_async_copy(eo_hbm.at[pl.ds(r, 1), :], rows.at[half, k, pl.ds(n, 1), :], sem.at[half])

    def issue_tile(tt, half):
        base = tt * tm

        def issue(n, c):
            for k in range(TOP_K):
                row_copy(half, n, k, pos_ref[(base + n) * TOP_K + k]).start(priority=k % 2)
            return c

        lax.fori_loop(0, tm, issue, 0)

    def drain(n, c):
        for k in range(TOP_K):
            row_copy(slot, n, k, 0).wait()
        return c

    @pl.when(tile == 0)
    def _():
        issue_tile(0, 0)

    @pl.when(tile + 1 < pl.num_programs(0))
    def _():
        issue_tile(tile + 1, 1 - slot)

    lax.fori_loop(0, tm, drain, 0)
    tg = tg_ref[...]
    out = tg[:, 0:1] * rows[slot, 0]
    for k in range(1, TOP_K):
        out = out + tg[:, k:k + 1] * rows[slot, k]
    y = h_ref[...] + out

    @pl.when(tile < tiles0)
    def _():
        y0_ref[...] = y

    @pl.when(tile >= tiles0)
    def _():
        y1_ref[...] = y


def _combine(h, tg, pos, eo, n0):
    n, d = h.shape
    tm = math.gcd(math.gcd(COMBINE_TILE, n0), n - n0)
    tiles0 = n0 // tm
    grid_spec = pltpu.PrefetchScalarGridSpec(
        num_scalar_prefetch=1,
        grid=(n // tm,),
        in_specs=[pl.BlockSpec((tm, d), lambda i, p: (i, 0)), pl.BlockSpec((tm, TOP_K), lambda i, p: (i, 0)),
                  pl.BlockSpec(memory_space=pl.ANY)],
        out_specs=[pl.BlockSpec((tm, d), lambda i, p: (jnp.minimum(i, tiles0 - 1), 0)),
                   pl.BlockSpec((tm, d), lambda i, p: (jnp.maximum(i - tiles0, 0), 0))],
        scratch_shapes=[pltpu.VMEM((2, TOP_K, tm, d), F32), pltpu.SemaphoreType.DMA((2,))],
    )
    return pl.pallas_call(
        functools.partial(_combine_body, tiles0=tiles0),
        grid_spec=grid_spec,
        out_shape=[jax.ShapeDtypeStruct((n0, d), F32), jax.ShapeDtypeStruct((n - n0, d), F32)],
        compiler_params=_params(("arbitrary",)),
        name="moe_combine",
    )(pos, h, tg, eo)


def _moe(h, xn, ti, tg, n0, w_gate, b_gate, w_up, b_up, w_down, b_down):
    n_exp = w_gate.shape[0]
    n_rows, pos, row_token, n_used, e_w, first, cnt = _moe_plan(ti, n_exp)
    xg = _dispatch(xn, row_token, n_used, n_rows)
    eo = _experts(xg, e_w, first, cnt, w_gate, b_gate, w_up, b_up, w_down, b_down)
    return _combine(h, tg, pos, eo, n0)


def _rope_tables(pos, rope, reps):
    half = rope // 2
    inv_freq = jnp.power(ROPE_THETA, -jnp.arange(half, dtype=F32) / half)
    ang = pos.astype(F32)[:, None] * inv_freq[None, :]
    cos, sin = jnp.cos(ang), jnp.sin(ang)
    return (jnp.tile(jnp.concatenate([cos, cos], axis=1), (1, reps)),
            jnp.tile(jnp.concatenate([-sin, sin], axis=1), (1, reps)))


def kernel(x_prompt, x_sample, cache_ckv, cache_kpe, cache_kscale, state_conv, state_h, page_table,
           norm_attn_gain, w_in, conv_w, conv_b, lru_w_a, lru_b_a, lru_w_i, lru_b_i, lru_lambda,
           q_norm_gain, w_uq, kv_norm_gain, w_uk, w_uv, qk_gain_q, qk_gain_k,
           out_norm_rg, out_norm_mla, w_out, norm_ffn_gain, w_router, b_router,
           w_gate, b_gate, w_up, b_up, w_down, b_down):
    nb, seq, d = x_prompt.shape
    db, t, _ = x_sample.shape
    assert t == SUBLANES, "the sample group is laid out one sequence per sublane group"
    lru_w = state_h.shape[1]
    kv_lora, nh, _ = w_uk.shape
    q_lora = w_uq.shape[0]
    rope = cache_kpe.shape[2]
    qk_head = QK_NOPE + rope
    n_p, n_s = nb * seq, db * t
    past_len = page_table.shape[1] * cache_ckv.shape[1]

    x_p, x_s = x_prompt.reshape(n_p, d), x_sample.reshape(n_s, d)
    pos = jnp.concatenate([jnp.tile(jnp.arange(seq), nb), jnp.tile(past_len + jnp.arange(t), db)])
    cos_w, sin_w = _rope_tables(pos, rope, nh)

    xrg, gate, ql, kvl, kr = _in_proj(x_p, x_s, norm_attn_gain, w_in.astype(BF16),
                                      (lru_w, lru_w, q_lora, kv_lora, rope))

    w_uq3 = w_uq.reshape(q_lora, nh, qk_head)
    wqn_bf = w_uq3[:, :, :QK_NOPE].reshape(q_lora, nh * QK_NOPE).astype(BF16)
    wqr_bf = w_uq3[:, :, QK_NOPE:].reshape(q_lora, nh * rope).astype(BF16)
    wukt_bf = jnp.transpose(w_uk, (1, 2, 0)).astype(BF16)
    wuk_bf = w_uk.reshape(kv_lora, nh * QK_NOPE).astype(BF16)
    qabs, qpe, c, kpe, ks = _mla_proj(
        ql, kvl, kr, cos_w, sin_w, q_norm_gain, wqn_bf, wqr_bf, qk_gain_q[:QK_NOPE],
        jnp.tile(qk_gain_q[QK_NOPE:], nh), qk_gain_k[:QK_NOPE], qk_gain_k[QK_NOPE:], kv_norm_gain,
        wukt_bf, wuk_bf, nh)

    wai_bf = jnp.concatenate([lru_w_a, lru_w_i], axis=-1).astype(BF16)
    y_p, h_p = _rec_prompt(xrg, gate, nb, seq, conv_w, conv_b, wai_bf, lru_b_a, lru_b_i, lru_lambda)
    xrg_s = xrg[n_p:].reshape(db, t, lru_w)
    xpad = jnp.concatenate([state_conv, xrg_s], axis=1)
    x_shift = [xpad[:, k:k + t].reshape(n_s, lru_w) for k in range(CONV_W)]
    y_s, h_s = _rec_sample(x_shift, gate, n_p, state_h, conv_w, conv_b, wai_bf, lru_b_a, lru_b_i, lru_lambda)
    p_conv = xrg[:n_p].reshape(nb, seq, lru_w)[:, seq - (CONV_W - 1):]
    s_conv = xpad[:, t:]

    kst = jnp.transpose(ks[:n_p])
    olat = _prompt_attention(qabs, qpe, c, kpe, kst, nb, seq)
    kst_new = jnp.swapaxes(ks[n_p:].reshape(db, t, nh), 1, 2)
    olat = _sample_attention(olat, qabs, qpe, c, kpe, kst_new, n_p, db, page_table, cache_ckv,
                             jnp.swapaxes(cache_kpe, 1, 2), jnp.swapaxes(cache_kscale, 1, 2))

    wuv_bf = jnp.transpose(w_uv, (1, 0, 2)).astype(BF16)
    hh, xn, ti, tg = _merge_route(x_p, x_s, y_p, y_s, olat,
                                  wuv_bf, out_norm_rg, out_norm_mla, w_out.astype(BF16), norm_ffn_gain,
                                  w_router.astype(BF16), b_router)
    y_p2, y_s2 = _moe(hh, xn, ti, tg, n_p, w_gate, b_gate, w_up, b_up, w_down, b_down)

    return (y_p2.reshape(nb, seq, d), y_s2.reshape(db, t, d),
            c[:n_p].reshape(nb, seq, kv_lora), kpe[:n_p].reshape(nb, seq, rope), ks[:n_p].reshape(nb, seq, nh),
            p_conv, h_p.reshape(nb, lru_w),
            c[n_p:].reshape(db, t, kv_lora), kpe[n_p:].reshape(db, t, rope), ks[n_p:].reshape(db, t, nh),
            s_conv, h_s)
```
